```python
import math
import jax, jax.numpy as jnp
from jax import lax
import numpy as np

D_MODEL = 1024
BATCH = 32
SEQ = 2048
DEPTH = 2

CTX_LEN = 256
GRID_W = 64
EPS = 1e-6

N_MIX_GROUPS = 4
GROUP_W = D_MODEL // N_MIX_GROUPS
D_MIX = N_MIX_GROUPS * GROUP_W

MLA_HEADS = 4
MLA_NOPE = 64
MLA_ROPE = 32
MLA_QK = MLA_NOPE + MLA_ROPE
MLA_V = GROUP_W // MLA_HEADS
MLA_Q_RANK = 3 * D_MODEL // 16
MLA_KV_RANK = D_MODEL // 8
ROPE_BASE = 10000.0
ATTN_BLOCK = 128

RET_HEADS = 4
RET_DK = 64
RET_DV = GROUP_W // RET_HEADS
RET_CHUNK = 128

HY_CH = GROUP_W
HY_CONV = 3
HY_BANDS = 16
HY_EMB = 1 + 2 * HY_BANDS
HY_FFN = 64
HY_FAST_PCT = 0.3
HY_SLOW_PCT = 1.5
HY_TARGET = 1e-2

LRU_W = GROUP_W
LRU_BLOCKS = 4
LRU_BW = LRU_W // LRU_BLOCKS
LRU_CONV = 4
LRU_C = 8.0

MOE_GROUPS = 4
MOE_PER_GROUP = 8
N_EXPERTS = MOE_GROUPS * MOE_PER_GROUP
TOP_K = 2
EXPERT_FF = D_MODEL // 2
MOE_BLOCK = 128

IN_SPLITS = (MLA_Q_RANK, MLA_KV_RANK, MLA_ROPE,
             RET_HEADS * RET_DK, RET_HEADS * RET_DK, GROUP_W, GROUP_W,
             3 * HY_CH, LRU_W, LRU_W)
IN_COLS = sum(IN_SPLITS)

kernel_name = 'hybrid_mla_retnet_hyena_rglru_hmoe_dit'


def rms_norm(x, g):
    xf = x.astype(jnp.float32)
    y = xf * lax.rsqrt(jnp.mean(xf * xf, axis=-1, keepdims=True) + EPS)
    return (y * g.astype(jnp.float32)).astype(x.dtype)


def modulate(h, shift, scale):
    return h * (1 + scale) + shift


def split_cols(z):
    out, o = [], 0
    for w in IN_SPLITS:
        out.append(z[..., o:o + w])
        o += w
    return out


def dwconv(x, w, b, pad_left):
    width, ch = w.shape
    y = lax.conv_general_dilated(x, w[:, None, :].astype(x.dtype), window_strides=(1,),
                                 padding=[(pad_left, width - 1 - pad_left)],
                                 dimension_numbers=('NWC', 'WIO', 'NWC'), feature_group_count=ch)
    return y + b.astype(x.dtype)


def axial_rope_angles(n_tokens):
    rows = n_tokens // GRID_W
    row = jnp.repeat(jnp.arange(rows), GRID_W).astype(jnp.float32)
    col = jnp.tile(jnp.arange(GRID_W), rows).astype(jnp.float32)
    half = MLA_ROPE // 4
    inv_freq = ROPE_BASE ** (-jnp.arange(half, dtype=jnp.float32) / half)
    ang = jnp.stack([row[:, None] * inv_freq, col[:, None] * inv_freq], axis=1)
    return jnp.cos(ang), jnp.sin(ang)


def apply_axial_rope(x, cos, sin):
    b, l, h, _ = x.shape
    xr = x.astype(jnp.float32).reshape(b, l, h, 2, 2, MLA_ROPE // 4)
    x1, x2 = xr[..., 0, :], xr[..., 1, :]
    cc, ss = cos[None, :, None], sin[None, :, None]
    out = jnp.stack([x1 * cc - x2 * ss, x2 * cc + x1 * ss], axis=-2)
    return out.reshape(b, l, h, MLA_ROPE).astype(x.dtype)


def rope_tail(t, cos, sin):
    return jnp.concatenate([t[..., :MLA_NOPE], apply_axial_rope(t[..., MLA_NOPE:], cos, sin)], axis=-1)


def mla_qkv(cq, ckv, kr, p):
    b, l, _ = cq.shape
    q = (rms_norm(cq, p['mla_q_norm_g']) @ p['mla_w_uq']).reshape(b, l, MLA_HEADS, MLA_QK)
    kv = (rms_norm(ckv, p['mla_kv_norm_g']) @ p['mla_w_ukv']).reshape(b, l, MLA_HEADS, MLA_NOPE + MLA_V)
    k_nope, v = kv[..., :MLA_NOPE], kv[..., MLA_NOPE:]
    k = jnp.concatenate([k_nope, jnp.broadcast_to(kr[:, :, None, :], (b, l, MLA_HEADS, MLA_ROPE))], axis=-1)
    return rms_norm(q, p['mla_qn_g']), rms_norm(k, p['mla_kn_g']), v


def softmax_attend(q, k, v):
    s = jnp.einsum('bqhd,bkhd->bhqk', q, k).astype(jnp.float32) * (MLA_QK ** -0.5)
    pr = jax.nn.softmax(s, axis=-1).astype(v.dtype)
    return jnp.einsum('bhqk,bkhd->bqhd', pr, v)


def blocked_attend(q, k, v):
    b, l, h, d = q.shape
    nb = l // ATTN_BLOCK
    qb = jnp.moveaxis(q.reshape(b, nb, ATTN_BLOCK, h, d), 1, 0)
    ob = lax.map(lambda qq: softmax_attend(qq, k, v), qb)
    return jnp.moveaxis(ob, 0, 1).reshape(b, l, h * v.shape[-1])


def ret_heads(zq, zk, zv):
    b, l, _ = zq.shape
    q = zq.astype(jnp.float32).reshape(b, l, RET_HEADS, RET_DK)
    k = zk.astype(jnp.float32).reshape(b, l, RET_HEADS, RET_DK) * (RET_DK ** -0.5)
    v = zv.astype(jnp.float32).reshape(b, l, RET_HEADS, RET_DV)
    return q, k, v


def retention_state(k, v, log_gamma):
    l = k.shape[1]
    w = jnp.exp((l - 1 - jnp.arange(l, dtype=jnp.float32))[:, None] * log_gamma)
    return jnp.einsum('blhk,blhv,lh->bhkv', k, v, w)


def retention_chunkwise(q, k, v, log_gamma, s0):
    b, l, h, dk = q.shape
    dv = v.shape[-1]
    n = l // RET_CHUNK
    j = jnp.arange(RET_CHUNK, dtype=jnp.float32)
    qc = q.reshape(b, n, RET_CHUNK, h, dk)
    kc = k.reshape(b, n, RET_CHUNK, h, dk)
    vc = v.reshape(b, n, RET_CHUNK, h, dv)
    rel = j[:, None] - j[None, :]
    dmask = jnp.where(rel[None] >= 0,
                      jnp.exp(jnp.maximum(rel, 0.0)[None] * log_gamma[:, None, None]), 0.0)
    inner = jnp.einsum('bnihk,bnjhk->bnhij', qc, kc) * dmask
    inner = jnp.einsum('bnhij,bnjhv->bnihv', inner, vc)
    k_dec = kc * jnp.exp((RET_CHUNK - 1 - j)[:, None] * log_gamma)[:, :, None]
    kv = jnp.einsum('bnjhk,bnjhv->nbhkv', k_dec, vc)
    chunk_decay = jnp.exp(RET_CHUNK * log_gamma)[None, :, None, None]

    def step(state, kv_i):
        return state * chunk_decay + kv_i, state

    _, prev = lax.scan(step, s0, kv)
    q_dec = qc * jnp.exp((j + 1)[:, None] * log_gamma)[:, :, None]
    cross = jnp.einsum('bnihk,nbhkv->bnihv', q_dec, prev)
    return (inner + cross).reshape(b, l, h, dv)


def retention_bidir(q, k, v, lg, s_fwd, s_bwd):
    fwd = retention_chunkwise(q, k, v, lg[0], s_fwd)
    bwd = retention_chunkwise(q[:, ::-1], k[:, ::-1], v[:, ::-1], lg[1], s_bwd)
    return fwd + bwd[:, ::-1]


def retention_out(o, zg, p):
    b, l = o.shape[0], o.shape[1]
    o = o * lax.rsqrt(jnp.mean(o * o, axis=-1, keepdims=True) + EPS)
    o = o.reshape(b, l, GROUP_W) * p['ret_norm_g'].astype(jnp.float32)
    return (o * jax.nn.silu(zg.astype(jnp.float32))).astype(zg.dtype)


def hyena_filter(n, p):
    f32 = jnp.float32
    t = jnp.linspace(0.0, 1.0, n, dtype=f32)[:, None]
    bands = jnp.linspace(1e-4, HY_BANDS - 1, HY_BANDS, dtype=f32)
    w = 2.0 * math.pi * jnp.arange(n, dtype=f32)[:, None] / n
    z = jnp.concatenate([t, jnp.cos(bands * w), -jnp.sin(bands * w)], axis=-1)
    freq = p['hy_freq'].astype(f32)
    hdn = jnp.sin(freq * (z @ p['hy_w1'].astype(f32) + p['hy_b1'].astype(f32)))
    hdn = jnp.sin(freq * (hdn @ p['hy_w2'].astype(f32) + p['hy_b2'].astype(f32)))
    filt = hdn @ p['hy_w3'].astype(f32)
    max_decay = math.log(HY_TARGET) / HY_FAST_PCT
    min_decay = math.log(HY_TARGET) / HY_SLOW_PCT
    deltas = jnp.abs(jnp.linspace(min_decay, max_decay, HY_CH, dtype=f32))
    window = jnp.exp(-t * deltas)
    h_fwd = filt[:, :HY_CH] * window
    h_bwd = filt[:, HY_CH:] * window
    return jnp.concatenate([h_fwd, jnp.zeros((1, HY_CH), f32), h_bwd[1:][::-1]], axis=0)


def hyena_mix(zh, p):
    l = zh.shape[1]
    u = dwconv(zh, p['hy_conv_w'], p['hy_conv_b'], HY_CONV // 2)
    x0, x1, v = jnp.split(u, 3, axis=-1)
    s = (x1 * v).astype(jnp.float32)
    hf = jnp.fft.rfft(hyena_filter(l, p), axis=0)
    y = jnp.fft.irfft(jnp.fft.rfft(s, n=2 * l, axis=1) * hf[None], n=2 * l, axis=1)[:, :l]
    y = y + s * p['hy_d'].astype(jnp.float32)
    return (x0.astype(jnp.float32) * y).astype(zh.dtype)


def rglru_scan(x, wa, ba, wx, bx, lam, h0):
    b, l, _ = x.shape
    xb = x.reshape(b, l, LRU_BLOCKS, LRU_BW)
    r = jax.nn.sigmoid(jnp.einsum('blnd,nde->blne', xb, wa).reshape(b, l, LRU_W) + ba)
    i = jax.nn.sigmoid(jnp.einsum('blnd,nde->blne', xb, wx).reshape(b, l, LRU_W) + bx)
    log_a = -LRU_C * r * jax.nn.softplus(-lam)
    a = jnp.exp(log_a)
    bt = jnp.sqrt(-jnp.expm1(2.0 * log_a)) * (i * x)
    bt = bt.at[:, 0].add(a[:, 0] * h0)

    def combine(lhs, rhs):
        a_l, b_l = lhs
        a_r, b_r = rhs
        return a_l * a_r, a_r * b_l + b_r

    _, h = lax.associative_scan(combine, (a, bt), axis=1)
    return h


def lru_dir_params(p, d):
    f = lambda t: t[d].astype(jnp.float32)
    return f(p['lru_wa']), f(p['lru_ba']), f(p['lru_wx']), f(p['lru_bx']), f(p['lru_lambda'])


def lru_out(h, zg):
    return (h * jax.nn.gelu(zg.astype(jnp.float32))).astype(zg.dtype)


def merge_groups(parts, p):
    y = jnp.concatenate(parts, axis=-1)
    shp = y.shape
    y = rms_norm(y.reshape(shp[:-1] + (N_MIX_GROUPS, GROUP_W)),
                 p['group_norm_g'].reshape(N_MIX_GROUPS, GROUP_W)).reshape(shp)
    return y @ p['w_out']


def parallel_mixer(a_lat, a_ctx, p, rope_cs, ctx_out):
    f32 = jnp.float32
    b = a_lat.shape[0]
    zl = split_cols(a_lat @ p['w_in'])
    zc = split_cols(a_ctx @ p['w_in'])
    cos, sin = rope_cs
    q_l, k_l, v_l = mla_qkv(zl[0], zl[1], zl[2], p)
    q_l, k_l = rope_tail(q_l, cos, sin), rope_tail(k_l, cos, sin)
    q_c, k_c, v_c = mla_qkv(zc[0], zc[1], zc[2], p)
    y_lat = [blocked_attend(q_l, jnp.concatenate([k_l, k_c], axis=1), jnp.concatenate([v_l, v_c], axis=1))]
    lg = p['ret_log_gamma'].astype(f32)
    rq_l, rk_l, rv_l = ret_heads(zl[3], zl[4], zl[5])
    rq_c, rk_c, rv_c = ret_heads(zc[3], zc[4], zc[5])
    s_f = retention_state(rk_c, rv_c, lg[0])
    s_b = retention_state(rk_c[:, ::-1], rv_c[:, ::-1], lg[1])
    y_lat.append(retention_out(retention_bidir(rq_l, rk_l, rv_l, lg, s_f, s_b), zl[6], p))
    y_lat.append(hyena_mix(zl[7], p))
    pf, pb = lru_dir_params(p, 0), lru_dir_params(p, 1)
    xc = dwconv(zc[8], p['lru_conv_w'], p['lru_conv_b'], LRU_CONV // 2).astype(f32)
    xl = dwconv(zl[8], p['lru_conv_w'], p['lru_conv_b'], LRU_CONV // 2).astype(f32)
    h0 = jnp.zeros((b, LRU_W), f32)
    hc_f = rglru_scan(xc, *pf, h0)
    hc_b = rglru_scan(xc[:, ::-1], *pb, h0)
    h_l = rglru_scan(xl, *pf, hc_f[:, -1]) + rglru_scan(xl[:, ::-1], *pb, hc_b[:, -1])[:, ::-1]
    y_lat.append(lru_out(h_l, zl[9]))
    out_lat = merge_groups(y_lat, p)
    if not ctx_out:
        return out_lat, None
    zero = jnp.zeros_like(s_f)
    n_ctx = a_ctx.shape[1]
    y_ctx = [softmax_attend(q_c, k_c, v_c).reshape(b, n_ctx, GROUP_W),
             retention_out(retention_bidir(rq_c, rk_c, rv_c, lg, zero, zero), zc[6], p),
             hyena_mix(zc[7], p),
             lru_out(hc_f + hc_b[:, ::-1], zc[9])]
    return out_lat, merge_groups(y_ctx, p)


def grouped_experts(h, eid, w, p):
    t, d = h.shape
    n_assign = t * TOP_K
    e_flat = eid.reshape(-1)
    w_flat = w.reshape(-1)
    tok = jnp.arange(n_assign, dtype=jnp.int32) // TOP_K
    counts = jnp.zeros((N_EXPERTS,), jnp.int32).at[e_flat].add(1)
    padded = (counts + MOE_BLOCK - 1) // MOE_BLOCK * MOE_BLOCK
    pad_end = jnp.cumsum(padded)
    pad_start = pad_end - padded
    raw_start = jnp.cumsum(counts) - counts
    order = jnp.argsort(e_flat)
    e_sorted = e_flat[order]
    dest = pad_start[e_sorted] + jnp.arange(n_assign, dtype=jnp.int32) - raw_start[e_sorted]
    n_blocks = -(-n_assign // MOE_BLOCK) + N_EXPERTS
    n_slots = n_blocks * MOE_BLOCK
    slot_tok = jnp.full((n_slots,), t, jnp.int32).at[dest].set(tok[order])
    slot_w = jnp.zeros((n_slots,), jnp.float32).at[dest].set(w_flat[order])
    block_exp = jnp.minimum(jnp.searchsorted(pad_end, jnp.arange(n_blocks, dtype=jnp.int32) * MOE_BLOCK,
                                             side='right'), N_EXPERTS - 1)
    h_pad = jnp.concatenate([h, jnp.zeros((1, d), h.dtype)], axis=0)
    xs = h_pad[slot_tok].reshape(n_blocks, MOE_BLOCK, d)

    def expert_block(args):
        xb, e = args
        return (jax.nn.silu(xb @ p['moe_w1'][e]) * (xb @ p['moe_w3'][e])) @ p['moe_w2'][e]

    ys = lax.map(expert_block, (xs, block_exp)).reshape(n_slots, d)
    out = jnp.zeros((t + 1, d), jnp.float32).at[slot_tok].add(ys.astype(jnp.float32) * slot_w[:, None])
    return out[:t].astype(h.dtype)


def hier_moe(h, p):
    t = h.shape[0]
    g_prob = jax.nn.softmax((h @ p['moe_w_group']).astype(jnp.float32), axis=-1)
    g_val, g_idx = lax.top_k(g_prob, 1)
    e_logits = (h @ p['moe_w_expert']).astype(jnp.float32).reshape(t, MOE_GROUPS, MOE_PER_GROUP)
    e_in_group = e_logits[jnp.arange(t), g_idx[:, 0]]
    e_val, e_idx = lax.top_k(e_in_group, TOP_K)
    w = g_val * jax.nn.softmax(e_val, axis=-1)
    eid = g_idx * MOE_PER_GROUP + e_idx
    return grouped_experts(h, eid, w, p)


def trunk_layer(x, ctx, c, c_ctx, p, rope_cs, ctx_out):
    b, s, d = x.shape
    n_ctx = ctx.shape[1]
    mod_l = (jax.nn.silu(c) @ p['w_mod'] + p['b_mod'])[:, None, :]
    mod_c = jax.nn.silu(c_ctx) @ p['w_mod'] + p['b_mod']
    sh1_l, sc1_l, g1_l, sh2_l, sc2_l, g2_l = jnp.split(mod_l, 6, axis=-1)
    sh1_c, sc1_c, g1_c, sh2_c, sc2_c, g2_c = jnp.split(mod_c, 6, axis=-1)
    a_l = modulate(rms_norm(x, p['norm1_g']), sh1_l, sc1_l)
    a_c = modulate(rms_norm(ctx, p['norm1_g']), sh1_c, sc1_c)
    m_l, m_c = parallel_mixer(a_l, a_c, p, rope_cs, ctx_out)
    x = x + g1_l * m_l
    f_l = modulate(rms_norm(x, p['norm2_g']), sh2_l, sc2_l).reshape(b * s, d)
    if ctx_out:
        ctx = ctx + g1_c * m_c
        f_c = modulate(rms_norm(ctx, p['norm2_g']), sh2_c, sc2_c).reshape(b * n_ctx, d)
        ff = hier_moe(jnp.concatenate([f_l, f_c], axis=0), p)
        x = x + g2_l * ff[:b * s].reshape(b, s, d)
        ctx = ctx + g2_c * ff[b * s:].reshape(b, n_ctx, d)
    else:
        x = x + g2_l * hier_moe(f_l, p).reshape(b, s, d)
    return x, ctx


def setup_inputs(seed: int = 0) -> dict:
    key = jax.random.key(seed)
    keys = list(jax.random.split(key, 48))

    def nrm(shape, scale):
        return scale * jax.random.normal(keys.pop(), shape, jnp.float32)

    def gain(shape):
        return 1.0 + 0.05 * jax.random.normal(keys.pop(), shape, jnp.float32)

    L = DEPTH
    x = nrm((BATCH, SEQ, D_MODEL), 1.0)
    c = nrm((BATCH, D_MODEL), 1.0)
    ctx = nrm((BATCH, CTX_LEN, D_MODEL), 1.0)
    c_ctx = nrm((D_MODEL,), 1.0)
    w_mod = nrm((L, D_MODEL, 6 * D_MODEL), 0.3 * D_MODEL ** -0.5)
    b_mod = nrm((L, 6 * D_MODEL), 0.02)
    norm1_g = gain((L, D_MODEL))
    norm2_g = gain((L, D_MODEL))
    w_in = nrm((L, D_MODEL, IN_COLS), D_MODEL ** -0.5)
    mla_q_norm_g = gain((L, MLA_Q_RANK))
    mla_w_uq = nrm((L, MLA_Q_RANK, MLA_HEADS * MLA_QK), MLA_Q_RANK ** -0.5)
    mla_kv_norm_g = gain((L, MLA_KV_RANK))
    mla_w_ukv = nrm((L, MLA_KV_RANK, MLA_HEADS * (MLA_NOPE + MLA_V)), MLA_KV_RANK ** -0.5)
    mla_qn_g = gain((L, MLA_QK))
    mla_kn_g = gain((L, MLA_QK))
    base_lg = jnp.log1p(-jnp.exp2(-5.0 - jnp.arange(RET_HEADS, dtype=jnp.float32)))
    ret_log_gamma = base_lg * gain((L, 2, RET_HEADS))
    ret_norm_g = gain((L, GROUP_W))
    hy_conv_w = nrm((L, HY_CONV, 3 * HY_CH), HY_CONV ** -0.5)
    hy_conv_b = nrm((L, 3 * HY_CH), 0.02)
    hy_w1 = nrm((L, HY_EMB, HY_FFN), HY_EMB ** -0.5)
    hy_b1 = nrm((L, HY_FFN), 0.02)
    hy_w2 = nrm((L, HY_FFN, HY_FFN), HY_FFN ** -0.5)
    hy_b2 = nrm((L, HY_FFN), 0.02)
    hy_w3 = nrm((L, HY_FFN, 2 * HY_CH), 0.1 * HY_FFN ** -0.5)
    hy_freq = gain((L, HY_FFN))
    hy_d = nrm((L, HY_CH), 0.5)
    lru_conv_w = nrm((L, LRU_CONV, LRU_W), LRU_CONV ** -0.5)
    lru_conv_b = nrm((L, LRU_W), 0.02)
    lru_wa = nrm((L, 2, LRU_BLOCKS, LRU_BW, LRU_BW), LRU_BW ** -0.5)
    lru_ba = nrm((L, 2, LRU_W), 0.02)
    lru_wx = nrm((L, 2, LRU_BLOCKS, LRU_BW, LRU_BW), LRU_BW ** -0.5)
    lru_bx = nrm((L, 2, LRU_W), 0.02)
    u = jax.random.uniform(keys.pop(), (L, 2, LRU_W), jnp.float32, 0.9, 0.999)
    a0 = u ** (1.0 / LRU_C)
    lru_lambda = jnp.log(a0) - jnp.log1p(-a0)
    group_norm_g = gain((L, D_MIX))
    w_out = nrm((L, D_MIX, D_MODEL), D_MIX ** -0.5)
    moe_w_group = nrm((L, D_MODEL, MOE_GROUPS), D_MODEL ** -0.5)
    moe_w_expert = nrm((L, D_MODEL, N_EXPERTS), D_MODEL ** -0.5)
    moe_w1 = nrm((L, N_EXPERTS, D_MODEL, EXPERT_FF), D_MODEL ** -0.5)
    moe_w3 = nrm((L, N_EXPERTS, D_MODEL, EXPERT_FF), D_MODEL ** -0.5)
    moe_w2 = nrm((L, N_EXPERTS, EXPERT_FF, D_MODEL), EXPERT_FF ** -0.5)
    return {'x': x, 'c': c, 'ctx': ctx, 'c_ctx': c_ctx, 'w_mod': w_mod, 'b_mod': b_mod,
            'norm1_g': norm1_g, 'norm2_g': norm2_g, 'w_in': w_in,
            'mla_q_norm_g': mla_q_norm_g, 'mla_w_uq': mla_w_uq, 'mla_kv_norm_g': mla_kv_norm_g,
            'mla_w_ukv': mla_w_ukv, 'mla_qn_g': mla_qn_g, 'mla_kn_g': mla_kn_g,
            'ret_log_gamma': ret_log_gamma, 'ret_norm_g': ret_norm_g,
            'hy_conv_w': hy_conv_w, 'hy_conv_b': hy_conv_b, 'hy_w1': hy_w1, 'hy_b1': hy_b1,
            'hy_w2': hy_w2, 'hy_b2': hy_b2, 'hy_w3': hy_w3, 'hy_freq': hy_freq, 'hy_d': hy_d,
            'lru_conv_w': lru_conv_w, 'lru_conv_b': lru_conv_b, 'lru_wa': lru_wa, 'lru_ba': lru_ba,
            'lru_wx': lru_wx, 'lru_bx': lru_bx, 'lru_lambda': lru_lambda,
            'group_norm_g': group_norm_g, 'w_out': w_out,
            'moe_w_group': moe_w_group, 'moe_w_expert': moe_w_expert,
            'moe_w1': moe_w1, 'moe_w3': moe_w3, 'moe_w2': moe_w2}


def reference(x, c, ctx, c_ctx, w_mod, b_mod, norm1_g, norm2_g, w_in,
              mla_q_norm_g, mla_w_uq, mla_kv_norm_g, mla_w_ukv, mla_qn_g, mla_kn_g,
              ret_log_gamma, ret_norm_g,
              hy_conv_w, hy_conv_b, hy_w1, hy_b1, hy_w2, hy_b2, hy_w3, hy_freq, hy_d,
              lru_conv_w, lru_conv_b, lru_wa, lru_ba, lru_wx, lru_bx, lru_lambda,
              group_norm_g, w_out, moe_w_group, moe_w_expert, moe_w1, moe_w3, moe_w2):
    rope_cs = axial_rope_angles(x.shape[1])
    for l in range(DEPTH):
        p = dict(w_mod=w_mod[l], b_mod=b_mod[l], norm1_g=norm1_g[l], norm2_g=norm2_g[l], w_in=w_in[l],
                 mla_q_norm_g=mla_q_norm_g[l], mla_w_uq=mla_w_uq[l], mla_kv_norm_g=mla_kv_norm_g[l],
                 mla_w_ukv=mla_w_ukv[l], mla_qn_g=mla_qn_g[l], mla_kn_g=mla_kn_g[l],
                 ret_log_gamma=ret_log_gamma[l], ret_norm_g=ret_norm_g[l],
                 hy_conv_w=hy_conv_w[l], hy_conv_b=hy_conv_b[l], hy_w1=hy_w1[l], hy_b1=hy_b1[l],
                 hy_w2=hy_w2[l], hy_b2=hy_b2[l], hy_w3=hy_w3[l], hy_freq=hy_freq[l], hy_d=hy_d[l],
                 lru_conv_w=lru_conv_w[l], lru_conv_b=lru_conv_b[l], lru_wa=lru_wa[l], lru_ba=lru_ba[l],
                 lru_wx=lru_wx[l], lru_bx=lru_bx[l], lru_lambda=lru_lambda[l],
                 group_norm_g=group_norm_g[l], w_out=w_out[l],
                 moe_w_group=moe_w_group[l], moe_w_expert=moe_w_expert[l],
                 moe_w1=moe_w1[l], moe_w3=moe_w3[l], moe_w2=moe_w2[l])
        x, ctx = trunk_layer(x, ctx, c, c_ctx, p, rope_cs, l < DEPTH - 1)
    return x
```

```python
import functools
import math

import jax
import jax.numpy as jnp
from jax import lax
from jax.experimental import pallas as pl
from jax.experimental.pallas import tpu as pltpu

F32 = jnp.float32
BF16 = jnp.bfloat16

D_MODEL = 1024
EPS = 1e-6
GRID_W = 64
N_GROUPS = 4
GROUP_W = 256

MLA_HEADS = 4
MLA_NOPE = 64
MLA_ROPE = 32
MLA_QK = 96
MLA_V = 64
MLA_Q_RANK = 192
MLA_KV_RANK = 128
ROPE_BASE = 10000.0

RET_HEADS = 4
RET_DK = 64

HY_BANDS = 16
HY_FAST_PCT = 0.3
HY_SLOW_PCT = 1.5
HY_TARGET = 1e-2

LRU_C = 8.0

MOE_GROUPS = 4
MOE_PER_GROUP = 8
N_EXPERTS = 32
TOP_K = 2
EXPERT_FF = 512

LANES = 128
SUBLANES = 8
MXU_DIM = 256

TM = 256
MOE_BM = 512
HY_CB = 8
VMEM_LIMIT = 56 * 1024 * 1024

Z_MLA, Z_RET, Z_HY, Z_LRU = 384, 1024, 768, 512
Z_COLS = Z_MLA + Z_RET + Z_HY + Z_LRU


def _cparams(sem):
    return pltpu.CompilerParams(dimension_semantics=sem, vmem_limit_bytes=VMEM_LIMIT)


def _bdot(a, b):
    return jnp.dot(a.astype(BF16), b.astype(BF16), preferred_element_type=F32)


def _bdot_t(a, b):
    return lax.dot_general(a.astype(BF16), b.astype(BF16), (((1,), (1,)), ((), ())),
                           preferred_element_type=F32)


def _fdot(a, b):
    return jnp.dot(a, b, preferred_element_type=F32, precision=lax.Precision.HIGHEST)


def _shift_rows(x, d):
    n = x.shape[0]
    r = pltpu.roll(x, (n - d) % n, 0)
    row = lax.broadcasted_iota(jnp.int32, (n, 1), 0)
    ok = (row + d >= 0) & (row + d < n)
    return jnp.where(ok, r, 0.0)


def _mod_kernel(c_ref, w_ref, b_ref, o_ref):
    c = c_ref[...]
    a = c * jax.nn.sigmoid(c)
    o_ref[0] = _fdot(a, w_ref[0]) + b_ref[0]


def _modulation(cc, w_mod, b_mod):
    nl, d, n6 = w_mod.shape
    rows = cc.shape[0]
    tn = 1536
    return pl.pallas_call(
        _mod_kernel,
        out_shape=jax.ShapeDtypeStruct((nl, rows, n6), F32),
        grid=(nl, n6 // tn),
        in_specs=[pl.BlockSpec((rows, d), lambda l, j: (0, 0)),
                  pl.BlockSpec((1, d, tn), lambda l, j: (l, 0, j)),
                  pl.BlockSpec((1, 1, tn), lambda l, j: (l, 0, j))],
        out_specs=pl.BlockSpec((1, rows, tn), lambda l, j: (l, 0, j)),
        compiler_params=_cparams(("arbitrary", "arbitrary")),
        name="modulation",
    )(cc, w_mod, b_mod.reshape(nl, 1, n6))


def _inproj_kernel(x_ref, mod_ref, g_ref, w_ref, zm_ref, zr_ref, zh_ref, zl_ref):
    x = x_ref[...]
    ms = jnp.mean(x * x, axis=-1, keepdims=True)
    y = x * lax.rsqrt(ms + EPS) * g_ref[...]
    sh = mod_ref[0, :, 0:D_MODEL]
    sc = mod_ref[0, :, D_MODEL:2 * D_MODEL]
    a = y * (1.0 + sc) + sh
    z = _bdot(a, w_ref[...])
    o = 0
    for ref, w in ((zm_ref, Z_MLA), (zr_ref, Z_RET), (zh_ref, Z_HY), (zl_ref, Z_LRU)):
        ref[...] = z[:, o:o + w].astype(BF16)
        o += w


def _mod_index(nt_all, n_lat, n_batch):
    def idx(b, t):
        return (jnp.where(t < n_lat, b, n_batch), 0, 0)
    return idx


def _inproj(xx, mod, g1, w_in_p, n_batch, nt, n_lat):
    rows = xx.shape[0]
    row_map = lambda b, t: (b * nt + t, 0)
    outs = [jax.ShapeDtypeStruct((rows, w), BF16) for w in (Z_MLA, Z_RET, Z_HY, Z_LRU)]
    return pl.pallas_call(
        _inproj_kernel,
        out_shape=outs,
        grid=(n_batch, nt),
        in_specs=[pl.BlockSpec((TM, D_MODEL), row_map),
                  pl.BlockSpec((1, 1, 6 * D_MODEL), _mod_index(nt, n_lat, n_batch)),
                  pl.BlockSpec((1, D_MODEL), lambda b, t: (0, 0)),
                  pl.BlockSpec((D_MODEL, Z_COLS), lambda b, t: (0, 0))],
        out_specs=[pl.BlockSpec((TM, w), row_map) for w in (Z_MLA, Z_RET, Z_HY, Z_LRU)],
        compiler_params=_cparams(("parallel", "arbitrary")),
        name="inproj",
    )(xx, mod, g1, w_in_p)


def _mla_prep_kernel(z_ref, cos_ref, sin_ref, gq_ref, gkv_ref, gqm_ref, gqs_ref, gkm_ref,
                     gkr_ref, gks_ref, wq_ref, wkv_ref, q_ref, k_ref, v_ref):
    z = z_ref[0].astype(F32)
    za = z[:, 0:2 * LANES]
    zb = z[:, 2 * LANES:3 * LANES]
    lane2 = lax.broadcasted_iota(jnp.int32, (1, 2 * LANES), 1)
    ms_q = jnp.sum(jnp.where(lane2 < MLA_Q_RANK, za * za, 0.0), axis=-1, keepdims=True) / MLA_Q_RANK
    cqn = za * lax.rsqrt(ms_q + EPS) * gq_ref[...]
    qall = _bdot(cqn, wq_ref[...])
    ms_kv = jnp.mean(zb * zb, axis=-1, keepdims=True)
    ckvn = zb * lax.rsqrt(ms_kv + EPS) * gkv_ref[...]
    kv = _bdot(ckvn, wkv_ref[...])

    cos = cos_ref[...]
    sin = sin_ref[...]
    col1 = za[:, LANES:2 * LANES]
    lane = lax.broadcasted_iota(jnp.int32, (1, LANES), 1)
    is_rope = (lane >= MLA_NOPE) & (lane < MLA_QK)
    kr = jnp.where(is_rope, col1, 0.0)
    kr_sw = jnp.where(is_rope, pltpu.roll(col1, LANES - MLA_ROPE, 1), 0.0)
    k_rot = kr * gkr_ref[...] * cos + kr_sw * gks_ref[...] * sin
    ss_kr = jnp.sum(kr * kr, axis=-1, keepdims=True)
    hw = MLA_HEADS * LANES
    for h in range(MLA_HEADS):
        qh = qall[:, h * LANES:(h + 1) * LANES]
        qs = qall[:, hw + h * LANES:hw + (h + 1) * LANES]
        rs = lax.rsqrt(jnp.sum(qh * qh, axis=-1, keepdims=True) / MLA_QK + EPS) * (MLA_QK ** -0.5)
        q_ref[0, :, h * LANES:(h + 1) * LANES] = (
            rs * (qh * gqm_ref[...] * cos + qs * gqs_ref[...] * sin)).astype(BF16)
        kn = kv[:, h * LANES:(h + 1) * LANES]
        rk = lax.rsqrt((jnp.sum(kn * kn, axis=-1, keepdims=True) + ss_kr) / MLA_QK + EPS)
        k_ref[0, :, h * LANES:(h + 1) * LANES] = (rk * (kn * gkm_ref[...] + k_rot)).astype(BF16)
    v_ref[0] = kv[:, hw:].astype(BF16)


def _mla_prep(zm, cos_t, sin_t, gains, wq, wkv, n_batch, nt):
    hw = MLA_HEADS * LANES
    vw = MLA_HEADS * GROUP_W
    seq = nt * TM
    small = lambda w: pl.BlockSpec((1, w), lambda b, t: (0, 0))
    blk = lambda w: pl.BlockSpec((1, TM, w), lambda b, t: (b, t, 0))
    return pl.pallas_call(
        _mla_prep_kernel,
        out_shape=[jax.ShapeDtypeStruct((n_batch, seq, hw), BF16),
                   jax.ShapeDtypeStruct((n_batch, seq, hw), BF16),
                   jax.ShapeDtypeStruct((n_batch, seq, vw), BF16)],
        grid=(n_batch, nt),
        in_specs=[blk(Z_MLA),
                  pl.BlockSpec((TM, LANES), lambda b, t: (t, 0)),
                  pl.BlockSpec((TM, LANES), lambda b, t: (t, 0)),
                  small(2 * LANES), small(LANES), small(LANES), small(LANES), small(LANES),
                  small(LANES), small(LANES),
                  pl.BlockSpec((2 * LANES, 2 * hw), lambda b, t: (0, 0)),
                  pl.BlockSpec((LANES, hw + vw), lambda b, t: (0, 0))],
        out_specs=[blk(hw), blk(hw), blk(vw)],
        compiler_params=_cparams(("parallel", "arbitrary")),
        name="mla_prep",
    )(zm, cos_t, sin_t, *gains, wq, wkv)


def _attn_kernel(n_lat, q_ref, k_ref, v_ref, o_ref):
    t = pl.program_id(1)

    def attend(r0, n):
        acc = jnp.zeros((TM, GROUP_W), F32)
        for h in range(MLA_HEADS):
            qh = q_ref[0, :, h * LANES:(h + 1) * LANES]
            kh = k_ref[0, r0:r0 + n, h * LANES:(h + 1) * LANES]
            s = _bdot_t(qh, kh)
            m = jnp.max(s, axis=-1, keepdims=True)
            p = jnp.exp(s - m)
            l = jnp.sum(p, axis=-1, keepdims=True)
            vh = v_ref[0, r0:r0 + n, h * GROUP_W:(h + 1) * GROUP_W]
            acc = acc + _bdot(p, vh) * (1.0 / l)
        o_ref[0] = acc.astype(BF16)

    @pl.when(t < n_lat)
    def _():
        attend(0, k_ref.shape[1])

    @pl.when(t >= n_lat)
    def _():
        attend(n_lat * TM, k_ref.shape[1] - n_lat * TM)


def _attention(q, k, v, n_batch, nt, n_lat):
    seq = q.shape[1]
    hw = MLA_HEADS * LANES
    vw = MLA_HEADS * GROUP_W
    return pl.pallas_call(
        functools.partial(_attn_kernel, n_lat),
        out_shape=jax.ShapeDtypeStruct((n_batch, nt * TM, GROUP_W), BF16),
        grid=(n_batch, nt),
        in_specs=[pl.BlockSpec((1, TM, hw), lambda b, t: (b, t, 0)),
                  pl.BlockSpec((1, seq, hw), lambda b, t: (b, 0, 0)),
                  pl.BlockSpec((1, seq, vw), lambda b, t: (b, 0, 0))],
        out_specs=pl.BlockSpec((1, TM, GROUP_W), lambda b, t: (b, t, 0)),
        compiler_params=_cparams(("parallel", "arbitrary")),
        name="mla_attention",
    )(q, k, v)


def _ret_kernel(n_lat, ctx_out, z_ref, dm_ref, dq_ref, dk_ref, gc_ref, bd_ref, g_ref,
                o_ref, acc_ref, st_ref):
    lane = lax.broadcasted_iota(jnp.int32, (1, GROUP_W), 1)
    hmask = [(lane >= h * RET_DK) & (lane < (h + 1) * RET_DK) for h in range(RET_HEADS)]
    ones_bd = bd_ref[...].astype(BF16)

    def qkv(r0):
        q = z_ref[0, pl.ds(r0, TM), 0:GROUP_W]
        k = z_ref[0, pl.ds(r0, TM), GROUP_W:2 * GROUP_W].astype(F32) * (RET_DK ** -0.5)
        v = z_ref[0, pl.ds(r0, TM), 2 * GROUP_W:3 * GROUP_W]
        return q, k, v

    def inner(q, k, v):
        kb = k.astype(BF16)
        acc = jnp.zeros((TM, GROUP_W), F32)
        for h in range(RET_HEADS):
            qh = jnp.where(hmask[h], q, jnp.zeros_like(q))
            vh = jnp.where(hmask[h], v, jnp.zeros_like(v))
            s = _bdot_t(qh, kb) * dm_ref[h]
            acc = acc + _bdot(s, vh)
        return acc

    def state_update(d, k, v):
        kd = (k * dk_ref[d]).T
        st_ref[...] = st_ref[...] * gc_ref[d] + _bdot(kd, v) * bd_ref[...]

    def cross(d, q):
        return _bdot(q.astype(F32) * dq_ref[d], st_ref[...])

    def finish(r0, o):
        sq = o * o
        hi = sq.astype(BF16)
        lo = (sq - hi.astype(F32)).astype(BF16)
        ms = (jnp.dot(hi, ones_bd, preferred_element_type=F32)
              + jnp.dot(lo, ones_bd, preferred_element_type=F32)) / RET_DK
        gate = z_ref[0, pl.ds(r0, TM), 3 * GROUP_W:4 * GROUP_W].astype(F32)
        y = o * lax.rsqrt(ms + EPS) * g_ref[...] * (gate * jax.nn.sigmoid(gate))
        o_ref[0, pl.ds(r0, TM), :] = y.astype(BF16)

    c0 = n_lat * TM
    qc, kc, vc = qkv(c0)
    if ctx_out:
        finish(c0, inner(qc, kc, vc))
    else:
        o_ref[0, pl.ds(c0, TM), :] = jnp.zeros((TM, GROUP_W), BF16)

    st_ref[...] = jnp.zeros_like(st_ref)
    state_update(0, kc, vc)

    def fwd(n, carry):
        r0 = pl.multiple_of(n * TM, TM)
        q, k, v = qkv(r0)
        acc_ref[pl.ds(r0, TM), :] = inner(q, k, v) + cross(0, q)
        state_update(0, k, v)
        return carry

    lax.fori_loop(0, n_lat, fwd, 0)

    st_ref[...] = jnp.zeros_like(st_ref)
    state_update(1, kc, vc)

    def bwd(i, carry):
        n = n_lat - 1 - i
        r0 = pl.multiple_of(n * TM, TM)
        q, k, v = qkv(r0)
        finish(r0, acc_ref[pl.ds(r0, TM), :] + cross(1, q))
        state_update(1, k, v)
        return carry

    lax.fori_loop(0, n_lat, bwd, 0)


def _retention(zr, tabs, g, n_batch, n_lat, ctx_out):
    seq = zr.shape[1]
    dm, dq, dk, gc, bd = tabs
    full = lambda a: pl.BlockSpec(a.shape, lambda b: (0,) * a.ndim)
    return pl.pallas_call(
        functools.partial(_ret_kernel, n_lat, ctx_out),
        out_shape=jax.ShapeDtypeStruct((n_batch, seq, GROUP_W), BF16),
        grid=(n_batch,),
        in_specs=[pl.BlockSpec((1, seq, Z_RET), lambda b: (b, 0, 0)),
                  full(dm), full(dq), full(dk), full(gc), full(bd), full(g)],
        out_specs=pl.BlockSpec((1, seq, GROUP_W), lambda b: (b, 0, 0)),
        scratch_shapes=[pltpu.VMEM((n_lat * TM, GROUP_W), F32),
                        pltpu.VMEM((GROUP_W, GROUP_W), F32)],
        compiler_params=_cparams(("parallel",)),
        name="retention",
    )(zr, dm, dq, dk, gc, bd, g)


def _retention_tables(lg):
    j = jnp.arange(TM, dtype=F32)
    rel = j[:, None] - j[None, :]
    lf = lg[0][:, None, None]
    lb = lg[1][:, None, None]
    dm = (jnp.where(rel[None] >= 0, jnp.exp(jnp.maximum(rel, 0.0)[None] * lf), 0.0)
          + jnp.where(rel[None] <= 0, jnp.exp(jnp.maximum(-rel, 0.0)[None] * lb), 0.0))
    lane_f = jnp.repeat(lg[0], RET_DK)[None, :]
    lane_b = jnp.repeat(lg[1], RET_DK)[None, :]
    dq = jnp.stack([jnp.exp((j + 1)[:, None] * lane_f), jnp.exp((TM - j)[:, None] * lane_b)])
    dk = jnp.stack([jnp.exp((TM - 1 - j)[:, None] * lane_f), jnp.exp(j[:, None] * lane_b)])
    gc = jnp.stack([jnp.exp(TM * lane_f).reshape(GROUP_W, 1), jnp.exp(TM * lane_b).reshape(GROUP_W, 1)])
    hid = jnp.arange(GROUP_W) // RET_DK
    bd = (hid[:, None] == hid[None, :]).astype(F32)
    return dm, dq, dk, gc, bd


def _hy_pre_kernel(n_lat, z_ref, w_ref, b_ref, s_ref, x0_ref):
    def seg(r0, n):
        z = z_ref[0, r0:r0 + n, :].astype(F32)
        u = (_shift_rows(z, -1) * w_ref[0:1, :] + z * w_ref[1:2, :]
             + _shift_rows(z, 1) * w_ref[2:3, :] + b_ref[...])
        x0_ref[0, r0:r0 + n, :] = u[:, 0:GROUP_W].astype(BF16)
        s_ref[0, r0:r0 + n, :] = (u[:, GROUP_W:2 * GROUP_W] * u[:, 2 * GROUP_W:]).astype(BF16)

    seg(0, n_lat * TM)
    seg(n_lat * TM, z_ref.shape[1] - n_lat * TM)


def _hy_pre(zh, w, b, n_batch, n_lat):
    seq = zh.shape[1]
    return pl.pallas_call(
        functools.partial(_hy_pre_kernel, n_lat),
        out_shape=[jax.ShapeDtypeStruct((n_batch, seq, GROUP_W), BF16)] * 2,
        grid=(n_batch,),
        in_specs=[pl.BlockSpec((1, seq, Z_HY), lambda i: (i, 0, 0)),
                  pl.BlockSpec(w.shape, lambda i: (0, 0)),
                  pl.BlockSpec(b.shape, lambda i: (0, 0))],
        out_specs=[pl.BlockSpec((1, seq, GROUP_W), lambda i: (i, 0, 0))] * 2,
        compiler_params=_cparams(("parallel",)),
        name="hyena_pre",
    )(zh, w, b)


def _hy_filter_kernel(z_ref, win_ref, w1_ref, b1_ref, w2_ref, b2_ref, w3_ref, fr_ref, hf_ref, hb_ref):
    fr = fr_ref[...]
    h = jnp.sin(fr * (_fdot(z_ref[...], w1_ref[...]) + b1_ref[...]))
    h = jnp.sin(fr * (_fdot(h, w2_ref[...]) + b2_ref[...]))
    filt = _fdot(h, w3_ref[...])
    win = win_ref[...]
    hf_ref[...] = filt[:, 0:GROUP_W] * win
    hb_ref[...] = filt[:, GROUP_W:] * win


def _hy_filter(n, w1p, b1, w2, b2, w3, freq):
    t = jnp.linspace(0.0, 1.0, n, dtype=F32)[:, None]
    bands = jnp.linspace(1e-4, HY_BANDS - 1, HY_BANDS, dtype=F32)
    w = 2.0 * math.pi * jnp.arange(n, dtype=F32)[:, None] / n
    z = jnp.concatenate([t, jnp.cos(bands * w), -jnp.sin(bands * w)], axis=-1)
    z = jnp.pad(z, ((0, 0), (0, LANES - z.shape[1])))
    max_decay = math.log(HY_TARGET) / HY_FAST_PCT
    min_decay = math.log(HY_TARGET) / HY_SLOW_PCT
    deltas = jnp.abs(jnp.linspace(min_decay, max_decay, GROUP_W, dtype=F32))
    window = jnp.exp(-t * deltas)
    full = lambda a: pl.BlockSpec(a.shape, lambda i: (0, 0))
    rows = lambda wd: pl.BlockSpec((TM, wd), lambda i: (i, 0))
    return pl.pallas_call(
        _hy_filter_kernel,
        out_shape=[jax.ShapeDtypeStruct((n, GROUP_W), F32)] * 2,
        grid=(n // TM,),
        in_specs=[rows(LANES), rows(GROUP_W), full(w1p), full(b1), full(w2), full(b2), full(w3),
                  full(freq)],
        out_specs=[rows(GROUP_W)] * 2,
        compiler_params=_cparams(("arbitrary",)),
        name="hyena_filter",
    )(z, window, w1p, b1, w2, b2, w3, freq)


def _toeplitz_rows(hf, hb, nj):
    n = hf.shape[0]
    full = jnp.concatenate([hf, jnp.zeros((1, GROUP_W), F32), hb[1:][::-1]], axis=0)
    d = jnp.arange(-(nj - 1), nj)[:, None]
    m = jnp.arange(2 * TM)[None, :]
    lag = d * TM + jnp.where(m < TM, m, m - 2 * TM)
    idx = jnp.mod(lag, 2 * n)
    return jnp.transpose(full[idx], (2, 0, 1))


def _hy_conv_kernel(nj, nb, s_ref, w_ref, y_ref, acc_ref):
    def chan(c, carry):
        sc = s_ref[c].reshape(nj * nb, TM)
        acc_ref[...] = jnp.zeros_like(acc_ref)
        for di in range(2 * nj - 1):
            d = di - (nj - 1)
            w = w_ref[c, di:di + 1, :]
            rolled = pltpu.roll(jnp.broadcast_to(w, (TM, 2 * TM)), 0, 1, stride=1, stride_axis=0)
            toep = rolled[:, 0:TM].astype(BF16)
            if d >= 0:
                res = jnp.dot(sc[0:(nj - d) * nb], toep, preferred_element_type=F32)
                acc_ref[d * nb:nj * nb, :] += res
            else:
                res = jnp.dot(sc[(-d) * nb:nj * nb], toep, preferred_element_type=F32)
                acc_ref[0:(nj + d) * nb, :] += res
        y_ref[c] = acc_ref[...].reshape(nj, nb, TM).astype(BF16)
        return carry

    lax.fori_loop(0, HY_CB, chan, 0)


def _hy_conv(s_t, w_rows):
    ch, nj, nb, _ = s_t.shape
    nd = 2 * nj - 1
    return pl.pallas_call(
        functools.partial(_hy_conv_kernel, nj, nb),
        out_shape=jax.ShapeDtypeStruct(s_t.shape, BF16),
        grid=(ch // HY_CB,),
        in_specs=[pl.BlockSpec((HY_CB, nj, nb, TM), lambda i: (i, 0, 0, 0)),
                  pl.BlockSpec((HY_CB, nd, 2 * TM), lambda i: (i, 0, 0))],
        out_specs=pl.BlockSpec((HY_CB, nj, nb, TM), lambda i: (i, 0, 0, 0)),
        scratch_shapes=[pltpu.VMEM((nj * nb, TM), F32)],
        compiler_params=_cparams(("parallel",)),
        name="hyena_conv",
    )(s_t, w_rows)


def _lru_tile_scan(a, b, carry, reverse):
    row = lax.broadcasted_iota(jnp.int32, (SUBLANES, 1), 0)
    for k in (1, 2, 4):
        if reverse:
            ok = row < SUBLANES - k
            sh = SUBLANES - k
        else:
            ok = row >= k
            sh = k
        a_s = jnp.where(ok, pltpu.roll(a, sh, 0), 1.0)
        b_s = jnp.where(ok, pltpu.roll(b, sh, 0), 0.0)
        b = a * b_s + b
        a = a * a_s
    return b + a * carry


def _lru_kernel(n_lat, z_ref, cw_ref, cb_ref, wg_ref, bg_ref, sp_ref, o_ref,
                af_ref, bf_ref, ab_ref, bb_ref, h_ref):
    seq = z_ref.shape[1]
    s_lat = n_lat * TM

    def prep(r0, n):
        x = z_ref[0, r0:r0 + n, 0:GROUP_W].astype(F32)
        x = (_shift_rows(x, -2) * cw_ref[0:1, :] + _shift_rows(x, -1) * cw_ref[1:2, :]
             + x * cw_ref[2:3, :] + _shift_rows(x, 1) * cw_ref[3:4, :] + cb_ref[...])
        g = _bdot(x, wg_ref[...]) + bg_ref[...]
        for d, (a_ref, b_ref) in enumerate(((af_ref, bf_ref), (ab_ref, bb_ref))):
            r = jax.nn.sigmoid(g[:, (2 * d) * GROUP_W:(2 * d + 1) * GROUP_W])
            i = jax.nn.sigmoid(g[:, (2 * d + 1) * GROUP_W:(2 * d + 2) * GROUP_W])
            log_a = -LRU_C * r * sp_ref[d:d + 1, :]
            a_ref[r0:r0 + n, :] = jnp.exp(log_a)
            b_ref[r0:r0 + n, :] = jnp.sqrt(1.0 - jnp.exp(2.0 * log_a)) * (i * x)

    prep(0, s_lat)
    prep(s_lat, seq - s_lat)

    n_ctx_tiles = (seq - s_lat) // SUBLANES
    n_lat_tiles = s_lat // SUBLANES

    def fwd_tile(base):
        def body(i, carry):
            r0 = pl.multiple_of(base + i * SUBLANES, SUBLANES)
            h = _lru_tile_scan(af_ref[pl.ds(r0, SUBLANES), :], bf_ref[pl.ds(r0, SUBLANES), :],
                               carry, False)
            h_ref[pl.ds(r0, SUBLANES), :] = h
            return h[SUBLANES - 1:SUBLANES, :]
        return body

    carry = jnp.zeros((1, GROUP_W), F32)
    carry = lax.fori_loop(0, n_ctx_tiles, fwd_tile(s_lat), carry)
    lax.fori_loop(0, n_lat_tiles, fwd_tile(0), carry)

    def bwd_tile(base, n_tiles):
        def body(i, carry):
            r0 = pl.multiple_of(base + (n_tiles - 1 - i) * SUBLANES, SUBLANES)
            h = _lru_tile_scan(ab_ref[pl.ds(r0, SUBLANES), :], bb_ref[pl.ds(r0, SUBLANES), :],
                               carry, True)
            gate = z_ref[0, pl.ds(r0, SUBLANES), GROUP_W:2 * GROUP_W].astype(F32)
            y = (h + h_ref[pl.ds(r0, SUBLANES), :]) * jax.nn.gelu(gate, approximate=True)
            o_ref[0, pl.ds(r0, SUBLANES), :] = y.astype(BF16)
            return h[0:1, :]
        return body

    carry = jnp.zeros((1, GROUP_W), F32)
    carry = lax.fori_loop(0, n_ctx_tiles, bwd_tile(s_lat, n_ctx_tiles), carry)
    lax.fori_loop(0, n_lat_tiles, bwd_tile(0, n_lat_tiles), carry)


def _lru(zl, cw, cb, wg, bg, sp, n_batch, n_lat):
    seq = zl.shape[1]
    full = lambda a: pl.BlockSpec(a.shape, lambda i: (0,) * a.ndim)
    return pl.pallas_call(
        functools.partial(_lru_kernel, n_lat),
        out_shape=jax.ShapeDtypeStruct((n_batch, seq, GROUP_W), BF16),
        grid=(n_batch,),
        in_specs=[pl.BlockSpec((1, seq, Z_LRU), lambda i: (i, 0, 0)),
                  full(cw), full(cb), full(wg), full(bg), full(sp)],
        out_specs=pl.BlockSpec((1, seq, GROUP_W), lambda i: (i, 0, 0)),
        scratch_shapes=[pltpu.VMEM((seq, GROUP_W), F32)] * 5,
        compiler_params=_cparams(("parallel",)),
        name="rglru",
    )(zl, cw, cb, wg, bg, sp)


def _merge_kernel(x_ref, ya_ref, yb_ref, yc_ref, s_ref, x0_ref, yd_ref, hd_ref, gg_ref, wo_ref,
                  mod_ref, g2_ref, wr_ref, x1_ref, f_ref, lg_ref, cat_ref):
    s = s_ref[0].astype(F32)
    yh = x0_ref[0].astype(F32) * (yc_ref[0].astype(F32) + s * hd_ref[...])
    parts = (ya_ref[0].astype(F32), yb_ref[0].astype(F32), yh, yd_ref[0].astype(F32))
    for g, y in enumerate(parts):
        ms = jnp.mean(y * y, axis=-1, keepdims=True)
        gg = gg_ref[:, g * GROUP_W:(g + 1) * GROUP_W]
        cat_ref[:, g * GROUP_W:(g + 1) * GROUP_W] = (y * lax.rsqrt(ms + EPS) * gg).astype(BF16)
    m = jnp.dot(cat_ref[...], wo_ref[...], preferred_element_type=F32)
    gate1 = mod_ref[0, :, 2 * D_MODEL:3 * D_MODEL]
    x1 = x_ref[...] + gate1 * m
    x1_ref[...] = x1
    ms = jnp.mean(x1 * x1, axis=-1, keepdims=True)
    sh = mod_ref[0, :, 3 * D_MODEL:4 * D_MODEL]
    sc = mod_ref[0, :, 4 * D_MODEL:5 * D_MODEL]
    f = (x1 * lax.rsqrt(ms + EPS) * g2_ref[...]) * (1.0 + sc) + sh
    fb = f.astype(BF16)
    f_ref[...] = fb
    lg_ref[...] = jnp.dot(fb, wr_ref[...], preferred_element_type=F32)


def _merge(xx, ys, hd, gg, wo, mod, g2, wr, n_batch, nt_all, nt, n_lat):
    rows_all = xx.shape[0]
    rows_out = n_batch * nt * TM
    in_rows = lambda b, t: (b * nt_all + t, 0)
    out_rows = lambda b, t: (b * nt + t, 0)
    seq_blk = pl.BlockSpec((1, TM, GROUP_W), lambda b, t: (b, t, 0))
    small = lambda a: pl.BlockSpec(a.shape, lambda b, t: (0, 0))
    return pl.pallas_call(
        _merge_kernel,
        out_shape=[jax.ShapeDtypeStruct((rows_all, D_MODEL), F32),
                   jax.ShapeDtypeStruct((rows_out, D_MODEL), BF16),
                   jax.ShapeDtypeStruct((rows_out, LANES), F32)],
        grid=(n_batch, nt),
        in_specs=[pl.BlockSpec((TM, D_MODEL), in_rows)] + [seq_blk] * 6
                 + [small(hd), small(gg), small(wo),
                    pl.BlockSpec((1, 1, 6 * D_MODEL), _mod_index(nt_all, n_lat, n_batch)),
                    small(g2), small(wr)],
        out_specs=[pl.BlockSpec((TM, D_MODEL), in_rows),
                   pl.BlockSpec((TM, D_MODEL), out_rows),
                   pl.BlockSpec((TM, LANES), out_rows)],
        scratch_shapes=[pltpu.VMEM((TM, D_MODEL), BF16)],
        input_output_aliases={0: 0},
        compiler_params=_cparams(("parallel", "arbitrary")),
        name="merge_outproj",
    )(xx, *ys, hd, gg, wo, mod, g2, wr)


def _expert_kernel(be_ref, nv_ref, x_ref, w1_ref, w3_ref, w2_ref, sw_ref, o_ref):
    i = pl.program_id(0)

    @pl.when(i < nv_ref[0])
    def _():
        x = x_ref[...]
        a = jnp.dot(x, w1_ref[0], preferred_element_type=F32)
        b = jnp.dot(x, w3_ref[0], preferred_element_type=F32)
        h = (a * jax.nn.sigmoid(a)) * b
        y = jnp.dot(h.astype(BF16), w2_ref[0], preferred_element_type=F32)
        o_ref[...] = (y * sw_ref[...]).astype(BF16)

    @pl.when(i >= nv_ref[0])
    def _():
        o_ref[...] = jnp.zeros_like(o_ref)


def _experts(xs, slot_w, block_exp, n_valid, w1, w3, w2):
    n_slots = xs.shape[0]
    n_blocks = n_slots // MOE_BM
    grid_spec = pltpu.PrefetchScalarGridSpec(
        num_scalar_prefetch=2,
        grid=(n_blocks,),
        in_specs=[pl.BlockSpec((MOE_BM, D_MODEL), lambda i, be, nv: (i, 0)),
                  pl.BlockSpec((1, D_MODEL, EXPERT_FF), lambda i, be, nv: (be[i], 0, 0)),
                  pl.BlockSpec((1, D_MODEL, EXPERT_FF), lambda i, be, nv: (be[i], 0, 0)),
                  pl.BlockSpec((1, EXPERT_FF, D_MODEL), lambda i, be, nv: (be[i], 0, 0)),
                  pl.BlockSpec((MOE_BM, 1), lambda i, be, nv: (i, 0))],
        out_specs=pl.BlockSpec((MOE_BM, D_MODEL), lambda i, be, nv: (i, 0)),
    )
    return pl.pallas_call(
        _expert_kernel,
        out_shape=jax.ShapeDtypeStruct((n_slots, D_MODEL), BF16),
        grid_spec=grid_spec,
        compiler_params=_cparams(("arbitrary",)),
        name="moe_experts",
    )(block_exp, n_valid, xs, w1, w3, w2, slot_w)


def _combine_kernel(x_ref, ya_ref, yb_ref, mod_ref, o_ref):
    gate2 = mod_ref[0, :, 5 * D_MODEL:6 * D_MODEL]
    o_ref[...] = x_ref[...] + gate2 * (ya_ref[...].astype(F32) + yb_ref[...].astype(F32))


def _combine(xx, ya, yb, mod, n_batch, nt_all, nt, n_lat, in_place):
    in_rows = lambda b, t: (b * nt_all + t, 0)
    out_rows = lambda b, t: (b * nt + t, 0)
    o_rows = in_rows if in_place else out_rows
    rows = xx.shape[0] if in_place else n_batch * nt * TM
    return pl.pallas_call(
        _combine_kernel,
        out_shape=jax.ShapeDtypeStruct((rows, D_MODEL), F32),
        grid=(n_batch, nt),
        in_specs=[pl.BlockSpec((TM, D_MODEL), in_rows),
                  pl.BlockSpec((TM, D_MODEL), out_rows),
                  pl.BlockSpec((TM, D_MODEL), out_rows),
                  pl.BlockSpec((1, 1, 6 * D_MODEL), _mod_index(nt_all, n_lat, n_batch))],
        out_specs=pl.BlockSpec((TM, D_MODEL), o_rows),
        input_output_aliases={0: 0} if in_place else {},
        compiler_params=_cparams(("parallel", "arbitrary")),
        name="moe_combine",
    )(xx, ya, yb, mod)


def _route(logits):
    t = logits.shape[0]
    g_prob = jax.nn.softmax(logits[:, :MOE_GROUPS], axis=-1)
    g_val, g_idx = lax.top_k(g_prob, 1)
    e_logits = logits[:, MOE_GROUPS:MOE_GROUPS + N_EXPERTS].reshape(t, MOE_GROUPS, MOE_PER_GROUP)
    e_in_group = jnp.take_along_axis(e_logits, g_idx[:, :, None], axis=1)[:, 0]
    e_val, e_idx = lax.top_k(e_in_group, TOP_K)
    w = g_val * jax.nn.softmax(e_val, axis=-1)
    eid = g_idx * MOE_PER_GROUP + e_idx
    return eid.astype(jnp.int32), w


def _dispatch_plan(eid, w):
    t = eid.shape[0]
    n_assign = t * TOP_K
    e_flat = eid.reshape(-1)
    w_flat = w.reshape(-1)
    counts = jnp.zeros((N_EXPERTS,), jnp.int32).at[e_flat].add(1)
    padded = (counts + MOE_BM - 1) // MOE_BM * MOE_BM
    pad_end = jnp.cumsum(padded)
    pad_start = pad_end - padded
    raw_start = jnp.cumsum(counts) - counts
    order = jnp.argsort(e_flat)
    e_sorted = e_flat[order]
    dest = pad_start[e_sorted] + jnp.arange(n_assign, dtype=jnp.int32) - raw_start[e_sorted]
    n_blocks = -(-n_assign // MOE_BM) + N_EXPERTS
    n_slots = n_blocks * MOE_BM
    slot_tok = jnp.zeros((n_slots,), jnp.int32).at[dest].set((order // TOP_K).astype(jnp.int32))
    slot_w = jnp.zeros((n_slots,), F32).at[dest].set(w_flat[order])
    pos = jnp.zeros((n_assign,), jnp.int32).at[order].set(dest.astype(jnp.int32)).reshape(t, TOP_K)
    block_exp = jnp.minimum(
        jnp.searchsorted(pad_end, jnp.arange(n_blocks, dtype=jnp.int32) * MOE_BM, side='right'),
        N_EXPERTS - 1).astype(jnp.int32)
    n_valid = (pad_end[-1] // MOE_BM).astype(jnp.int32).reshape(1)
    return slot_tok, slot_w.reshape(n_slots, 1), pos, block_exp, n_valid


def _moe(f, logits, w1, w3, w2):
    eid, w = _route(logits)
    slot_tok, slot_w, pos, block_exp, n_valid = _dispatch_plan(eid, w)
    xs = jnp.take(f, slot_tok, axis=0)
    ys = _experts(xs, slot_w, block_exp, n_valid, w1, w3, w2)
    return jnp.take(ys, pos[:, 0], axis=0), jnp.take(ys, pos[:, 1], axis=0)


def _pack_w_in(w_in):
    o = 0
    cols = {}
    for name, wd in (('cq', 192), ('ckv', 128), ('kr', 32), ('ret', 1024), ('hy', 768), ('lru', 512)):
        cols[name] = w_in[:, o:o + wd]
        o += wd
    swap = jnp.arange(MLA_ROPE) ^ 8
    kr_sw = cols['kr'][:, swap]
    return jnp.concatenate([cols['cq'], cols['kr'], kr_sw, cols['ckv'], cols['ret'], cols['hy'],
                            cols['lru']], axis=1).astype(BF16)


def _pack_mla(w_uq, w_ukv, q_norm_g, kv_norm_g, qn_g, kn_g):
    hw = MLA_HEADS * LANES
    swap = jnp.arange(MLA_ROPE) ^ 8
    wq = jnp.zeros((2 * LANES, 2 * hw), F32)
    wkv = jnp.zeros((LANES, hw + MLA_HEADS * GROUP_W), F32)
    for h in range(MLA_HEADS):
        qh = w_uq[:, h * MLA_QK:(h + 1) * MLA_QK]
        wq = wq.at[:MLA_Q_RANK, h * LANES:h * LANES + MLA_QK].set(qh)
        wq = wq.at[:MLA_Q_RANK, hw + h * LANES + MLA_NOPE:hw + h * LANES + MLA_QK].set(
            qh[:, MLA_NOPE + swap])
        kvh = w_ukv[:, h * (MLA_NOPE + MLA_V):(h + 1) * (MLA_NOPE + MLA_V)]
        wkv = wkv.at[:, h * LANES:h * LANES + MLA_NOPE].set(kvh[:, :MLA_NOPE])
        v0 = hw + h * GROUP_W + h * MLA_V
        wkv = wkv.at[:, v0:v0 + MLA_V].set(kvh[:, MLA_NOPE:])
    pad = lambda v, lo, n: jnp.zeros((1, n), F32).at[0, lo:lo + v.shape[0]].set(v)
    gains = (pad(q_norm_g, 0, 2 * LANES), kv_norm_g.reshape(1, LANES),
             pad(qn_g, 0, LANES), pad(qn_g[MLA_NOPE + swap], MLA_NOPE, LANES),
             pad(kn_g[:MLA_NOPE], 0, LANES), pad(kn_g[MLA_NOPE:], MLA_NOPE, LANES),
             pad(kn_g[MLA_NOPE + swap], MLA_NOPE, LANES))
    return wq.astype(BF16), wkv.astype(BF16), gains


def _rope_tables(s_lat, seq):
    rows = s_lat // GRID_W
    row = jnp.repeat(jnp.arange(rows), GRID_W).astype(F32)
    col = jnp.tile(jnp.arange(GRID_W), rows).astype(F32)
    half = MLA_ROPE // 4
    inv_freq = ROPE_BASE ** (-jnp.arange(half, dtype=F32) / half)
    ar = row[:, None] * inv_freq
    ac = col[:, None] * inv_freq
    cos32 = jnp.concatenate([jnp.cos(ar), jnp.cos(ar), jnp.cos(ac), jnp.cos(ac)], axis=1)
    sin32 = jnp.concatenate([-jnp.sin(ar), jnp.sin(ar), -jnp.sin(ac), jnp.sin(ac)], axis=1)
    cos_t = jnp.ones((seq, LANES), F32).at[:s_lat, MLA_NOPE:MLA_QK].set(cos32)
    sin_t = jnp.zeros((seq, LANES), F32).at[:s_lat, MLA_NOPE:MLA_QK].set(sin32)
    return cos_t, sin_t


def _block_diag(w):
    nb, bw, _ = w.shape
    out = jnp.zeros((nb * bw, nb * bw), F32)
    for i in range(nb):
        out = out.at[i * bw:(i + 1) * bw, i * bw:(i + 1) * bw].set(w[i])
    return out


def kernel(x, c, ctx, c_ctx, w_mod, b_mod, norm1_g, norm2_g, w_in, mla_q_norm_g, mla_w_uq, mla_kv_norm_g, mla_w_ukv, mla_qn_g, mla_kn_g, ret_log_gamma, ret_norm_g, hy_conv_w, hy_conv_b, hy_w1, hy_b1, hy_w2, hy_b2, hy_w3, hy_freq, hy_d, lru_conv_w, lru_conv_b, lru_wa, lru_ba, lru_wx, lru_bx, lru_lambda, group_norm_g, w_out, moe_w_group, moe_w_expert, moe_w1, moe_w3, moe_w2):
    n_batch, s_lat, d = x.shape
    n_ctx = ctx.shape[1]
    depth = w_mod.shape[0]
    assert d == D_MODEL and n_ctx == TM and s_lat % TM == 0 and s_lat % GRID_W == 0
    n_lat = s_lat // TM
    nt_all = n_lat + 1
    seq = nt_all * TM
    mod_rows = -(-(n_batch + 1) // SUBLANES) * SUBLANES

    cc = jnp.zeros((mod_rows, d), F32).at[:n_batch].set(c).at[n_batch].set(c_ctx)
    mod_all = _modulation(cc, w_mod, b_mod)
    xx = jnp.concatenate([x, ctx], axis=1).reshape(n_batch * seq, d)
    cos_t, sin_t = _rope_tables(s_lat, seq)

    for l in range(depth):
        ctx_out = l < depth - 1
        nt = nt_all if ctx_out else n_lat
        mod = mod_all[l].reshape(mod_rows, 1, 6 * d)

        zm, zr, zh, zl = _inproj(xx, mod, norm1_g[l].reshape(1, d), _pack_w_in(w_in[l]),
                                 n_batch, nt_all, n_lat)
        zm = zm.reshape(n_batch, seq, Z_MLA)
        zr = zr.reshape(n_batch, seq, Z_RET)
        zh = zh.reshape(n_batch, seq, Z_HY)
        zl = zl.reshape(n_batch, seq, Z_LRU)

        wq, wkv, gains = _pack_mla(mla_w_uq[l], mla_w_ukv[l], mla_q_norm_g[l], mla_kv_norm_g[l],
                                   mla_qn_g[l], mla_kn_g[l])
        q, k, v = _mla_prep(zm, cos_t, sin_t, gains, wq, wkv, n_batch, nt_all)
        y_mla = _attention(q, k, v, n_batch, nt, n_lat)

        y_ret = _retention(zr, _retention_tables(ret_log_gamma[l].astype(F32)),
                           ret_norm_g[l].reshape(1, GROUP_W), n_batch, n_lat, ctx_out)

        s_hy, x0_hy = _hy_pre(zh, hy_conv_w[l], hy_conv_b[l].reshape(1, Z_HY), n_batch, n_lat)
        w1p = jnp.pad(hy_w1[l], ((0, LANES - hy_w1.shape[1]), (0, 0)))
        fargs = (w1p, hy_b1[l].reshape(1, -1), hy_w2[l], hy_b2[l].reshape(1, -1), hy_w3[l],
                 hy_freq[l].reshape(1, -1))
        hf, hb = _hy_filter(s_lat, *fargs)
        s_t = jnp.transpose(s_hy[:, :s_lat].reshape(n_batch, n_lat, TM, GROUP_W), (3, 1, 0, 2))
        y_t = _hy_conv(s_t, _toeplitz_rows(hf, hb, n_lat))
        y_hy = jnp.transpose(y_t, (2, 1, 3, 0)).reshape(n_batch, s_lat, GROUP_W)
        if ctx_out:
            hf_c, hb_c = _hy_filter(n_ctx, *fargs)
            sc_t = jnp.transpose(s_hy[:, s_lat:].reshape(n_batch, 1, TM, GROUP_W), (3, 1, 0, 2))
            yc_t = _hy_conv(sc_t, _toeplitz_rows(hf_c, hb_c, 1))
            y_hy_c = jnp.transpose(yc_t, (2, 1, 3, 0)).reshape(n_batch, n_ctx, GROUP_W)
        else:
            y_hy_c = jnp.zeros((n_batch, n_ctx, GROUP_W), BF16)
        y_hy = jnp.concatenate([y_hy, y_hy_c], axis=1)

        wg = jnp.concatenate([_block_diag(lru_wa[l, 0]), _block_diag(lru_wx[l, 0]),
                              _block_diag(lru_wa[l, 1]), _block_diag(lru_wx[l, 1])], axis=1).astype(BF16)
        bg = jnp.concatenate([lru_ba[l, 0], lru_bx[l, 0], lru_ba[l, 1], lru_bx[l, 1]]).reshape(1, -1)
        sp = jax.nn.softplus(-lru_lambda[l].astype(F32))
        y_lru = _lru(zl, lru_conv_w[l], lru_conv_b[l].reshape(1, -1), wg, bg, sp, n_batch, n_lat)

        wr = jnp.zeros((d, LANES), F32).at[:, :MOE_GROUPS].set(moe_w_group[l])
        wr = wr.at[:, MOE_GROUPS:MOE_GROUPS + N_EXPERTS].set(moe_w_expert[l]).astype(BF16)
        xx, f, logits = _merge(xx, (y_mla, y_ret, y_hy, s_hy, x0_hy, y_lru), hy_d[l].reshape(1, -1),
                               group_norm_g[l].reshape(1, -1), w_out[l].astype(BF16), mod,
                               norm2_g[l].reshape(1, d), wr, n_batch, nt_all, nt, n_lat)

        ya, yb = _moe(f, logits, moe_w1[l].astype(BF16), moe_w3[l].astype(BF16), moe_w2[l].astype(BF16))
        xx = _combine(xx, ya, yb, mod, n_batch, nt_all, nt, n_lat, in_place=ctx_out)

    return xx.reshape(n_batch, s_lat, d)
```

```python
import functools
import math

import jax
import jax.numpy as jnp
from jax import lax
from jax.experimental import pallas as pl
from jax.experimental.pallas import tpu as pltpu

F32 = jnp.float32
BF16 = jnp.bfloat16

D_MODEL = 1024
EPS = 1e-6
GRID_W = 64
N_GROUPS = 4
GROUP_W = 256

MLA_HEADS = 4
MLA_NOPE = 64
MLA_ROPE = 32
MLA_QK = 96
MLA_V = 64
MLA_Q_RANK = 192
MLA_KV_RANK = 128
ROPE_BASE = 10000.0

RET_HEADS = 4
RET_DK = 64

HY_BANDS = 16
HY_FAST_PCT = 0.3
HY_SLOW_PCT = 1.5
HY_TARGET = 1e-2

LRU_C = 8.0

MOE_GROUPS = 4
MOE_PER_GROUP = 8
N_EXPERTS = 32
TOP_K = 2
EXPERT_FF = 512

LANES = 128
SUBLANES = 8
MXU_DIM = 256

TM = 256
MOE_BM = 512
HY_CB = 8
VMEM_LIMIT = 56 * 1024 * 1024

Z_MLA, Z_RET, Z_HY, Z_LRU = 384, 1024, 768, 512
Z_COLS = Z_MLA + Z_RET + Z_HY + Z_LRU


def _cparams(sem):
    return pltpu.CompilerParams(dimension_semantics=sem, vmem_limit_bytes=VMEM_LIMIT)


def _bdot(a, b):
    return jnp.dot(a.astype(BF16), b.astype(BF16), preferred_element_type=F32)


def _bdot_t(a, b):
    return lax.dot_general(a.astype(BF16), b.astype(BF16), (((1,), (1,)), ((), ())),
                           preferred_element_type=F32)


def _fdot(a, b):
    return jnp.dot(a, b, preferred_element_type=F32, precision=lax.Precision.HIGHEST)


def _shift_rows(x, d):
    n = x.shape[0]
    r = pltpu.roll(x, (n - d) % n, 0)
    row = lax.broadcasted_iota(jnp.int32, (n, 1), 0)
    ok = (row + d >= 0) & (row + d < n)
    return jnp.where(ok, r, 0.0)


def _mod_kernel(c_ref, w_ref, b_ref, o_ref):
    c = c_ref[...]
    a = c * jax.nn.sigmoid(c)
    o_ref[0] = _fdot(a, w_ref[0]) + b_ref[0]


def _modulation(cc, w_mod, b_mod):
    nl, d, n6 = w_mod.shape
    rows = cc.shape[0]
    tn = 1536
    return pl.pallas_call(
        _mod_kernel,
        out_shape=jax.ShapeDtypeStruct((nl, rows, n6), F32),
        grid=(nl, n6 // tn),
        in_specs=[pl.BlockSpec((rows, d), lambda l, j: (0, 0)),
                  pl.BlockSpec((1, d, tn), lambda l, j: (l, 0, j)),
                  pl.BlockSpec((1, 1, tn), lambda l, j: (l, 0, j))],
        out_specs=pl.BlockSpec((1, rows, tn), lambda l, j: (l, 0, j)),
        compiler_params=_cparams(("arbitrary", "arbitrary")),
        name="modulation",
    )(cc, w_mod, b_mod.reshape(nl, 1, n6))


def _inproj_kernel(x_ref, mod_ref, g_ref, w_ref, zm_ref, zr_ref, zh_ref, zl_ref):
    x = x_ref[...]
    ms = jnp.mean(x * x, axis=-1, keepdims=True)
    y = x * lax.rsqrt(ms + EPS) * g_ref[...]
    sh = mod_ref[0, :, 0:D_MODEL]
    sc = mod_ref[0, :, D_MODEL:2 * D_MODEL]
    a = y * (1.0 + sc) + sh
    z = _bdot(a, w_ref[...])
    o = 0
    for ref, w in ((zm_ref, Z_MLA), (zr_ref, Z_RET), (zh_ref, Z_HY), (zl_ref, Z_LRU)):
        ref[...] = z[:, o:o + w].astype(BF16)
        o += w


def _mod_index(nt_all, n_lat, n_batch):
    def idx(b, t):
        return (jnp.where(t < n_lat, b, n_batch), 0, 0)
    return idx


def _inproj(xx, mod, g1, w_in_p, n_batch, nt, n_lat):
    rows = xx.shape[0]
    row_map = lambda b, t: (b * nt + t, 0)
    outs = [jax.ShapeDtypeStruct((rows, w), BF16) for w in (Z_MLA, Z_RET, Z_HY, Z_LRU)]
    return pl.pallas_call(
        _inproj_kernel,
        out_shape=outs,
        grid=(n_batch, nt),
        in_specs=[pl.BlockSpec((TM, D_MODEL), row_map),
                  pl.BlockSpec((1, 1, 6 * D_MODEL), _mod_index(nt, n_lat, n_batch)),
                  pl.BlockSpec((1, D_MODEL), lambda b, t: (0, 0)),
                  pl.BlockSpec((D_MODEL, Z_COLS), lambda b, t: (0, 0))],
        out_specs=[pl.BlockSpec((TM, w), row_map) for w in (Z_MLA, Z_RET, Z_HY, Z_LRU)],
        compiler_params=_cparams(("parallel", "arbitrary")),
        name="inproj",
    )(xx, mod, g1, w_in_p)


def _mla_prep_kernel(z_ref, cos_ref, sin_ref, gq_ref, gkv_ref, gqm_ref, gqs_ref, gkm_ref,
                     gkr_ref, gks_ref, wq_ref, wkv_ref, q_ref, k_ref, v_ref):
    z = z_ref[0].astype(F32)
    za = z[:, 0:2 * LANES]
    zb = z[:, 2 * LANES:3 * LANES]
    lane2 = lax.broadcasted_iota(jnp.int32, (1, 2 * LANES), 1)
    ms_q = jnp.sum(jnp.where(lane2 < MLA_Q_RANK, za * za, 0.0), axis=-1, keepdims=True) / MLA_Q_RANK
    cqn = za * lax.rsqrt(ms_q + EPS) * gq_ref[...]
    qall = _bdot(cqn, wq_ref[...])
    ms_kv = jnp.mean(zb * zb, axis=-1, keepdims=True)
    ckvn = zb * lax.rsqrt(ms_kv + EPS) * gkv_ref[...]
    kv = _bdot(ckvn, wkv_ref[...])

    cos = cos_ref[...]
    sin = sin_ref[...]
    col1 = za[:, LANES:2 * LANES]
    lane = lax.broadcasted_iota(jnp.int32, (1, LANES), 1)
    is_rope = (lane >= MLA_NOPE) & (lane < MLA_QK)
    kr = jnp.where(is_rope, col1, 0.0)
    kr_sw = jnp.where(is_rope, pltpu.roll(col1, LANES - MLA_ROPE, 1), 0.0)
    k_rot = kr * gkr_ref[...] * cos + kr_sw * gks_ref[...] * sin
    ss_kr = jnp.sum(kr * kr, axis=-1, keepdims=True)
    hw = MLA_HEADS * LANES
    for h in range(MLA_HEADS):
        qh = qall[:, h * LANES:(h + 1) * LANES]
        qs = qall[:, hw + h * LANES:hw + (h + 1) * LANES]
        rs = lax.rsqrt(jnp.sum(qh * qh, axis=-1, keepdims=True) / MLA_QK + EPS) * (MLA_QK ** -0.5)
        q_ref[0, :, h * LANES:(h + 1) * LANES] = (
            rs * (qh * gqm_ref[...] * cos + qs * gqs_ref[...] * sin)).astype(BF16)
        kn = kv[:, h * LANES:(h + 1) * LANES]
        rk = lax.rsqrt((jnp.sum(kn * kn, axis=-1, keepdims=True) + ss_kr) / MLA_QK + EPS)
        k_ref[0, :, h * LANES:(h + 1) * LANES] = (rk * (kn * gkm_ref[...] + k_rot)).astype(BF16)
    v_ref[0] = kv[:, hw:].astype(BF16)


def _mla_prep(zm, cos_t, sin_t, gains, wq, wkv, n_batch, nt):
    hw = MLA_HEADS * LANES
    vw = MLA_HEADS * GROUP_W
    seq = nt * TM
    small = lambda w: pl.BlockSpec((1, w), lambda b, t: (0, 0))
    blk = lambda w: pl.BlockSpec((1, TM, w), lambda b, t: (b, t, 0))
    return pl.pallas_call(
        _mla_prep_kernel,
        out_shape=[jax.ShapeDtypeStruct((n_batch, seq, hw), BF16),
                   jax.ShapeDtypeStruct((n_batch, seq, hw), BF16),
                   jax.ShapeDtypeStruct((n_batch, seq, vw), BF16)],
        grid=(n_batch, nt),
        in_specs=[blk(Z_MLA),
                  pl.BlockSpec((TM, LANES), lambda b, t: (t, 0)),
                  pl.BlockSpec((TM, LANES), lambda b, t: (t, 0)),
                  small(2 * LANES), small(LANES), small(LANES), small(LANES), small(LANES),
                  small(LANES), small(LANES),
                  pl.BlockSpec((2 * LANES, 2 * hw), lambda b, t: (0, 0)),
                  pl.BlockSpec((LANES, hw + vw), lambda b, t: (0, 0))],
        out_specs=[blk(hw), blk(hw), blk(vw)],
        compiler_params=_cparams(("parallel", "arbitrary")),
        name="mla_prep",
    )(zm, cos_t, sin_t, *gains, wq, wkv)


def _attn_kernel(n_lat, q_ref, k_ref, v_ref, o_ref):
    t = pl.program_id(1)

    def attend(r0, n):
        acc = jnp.zeros((TM, GROUP_W), F32)
        for h in range(MLA_HEADS):
            qh = q_ref[0, :, h * LANES:(h + 1) * LANES]
            kh = k_ref[0, r0:r0 + n, h * LANES:(h + 1) * LANES]
            s = _bdot_t(qh, kh)
            m = jnp.max(s, axis=-1, keepdims=True)
            p = jnp.exp(s - m)
            l = jnp.sum(p, axis=-1, keepdims=True)
            vh = v_ref[0, r0:r0 + n, h * GROUP_W:(h + 1) * GROUP_W]
            acc = acc + _bdot(p, vh) * (1.0 / l)
        o_ref[0] = acc.astype(BF16)

    @pl.when(t < n_lat)
    def _():
        attend(0, k_ref.shape[1])

    @pl.when(t >= n_lat)
    def _():
        attend(n_lat * TM, k_ref.shape[1] - n_lat * TM)


def _attention(q, k, v, n_batch, nt, n_lat):
    seq = q.shape[1]
    hw = MLA_HEADS * LANES
    vw = MLA_HEADS * GROUP_W
    return pl.pallas_call(
        functools.partial(_attn_kernel, n_lat),
        out_shape=jax.ShapeDtypeStruct((n_batch, nt * TM, GROUP_W), BF16),
        grid=(n_batch, nt),
        in_specs=[pl.BlockSpec((1, TM, hw), lambda b, t: (b, t, 0)),
                  pl.BlockSpec((1, seq, hw), lambda b, t: (b, 0, 0)),
                  pl.BlockSpec((1, seq, vw), lambda b, t: (b, 0, 0))],
        out_specs=pl.BlockSpec((1, TM, GROUP_W), lambda b, t: (b, t, 0)),
        compiler_params=_cparams(("parallel", "arbitrary")),
        name="mla_attention",
    )(q, k, v)


def _ret_kernel(n_lat, ctx_out, z_ref, dm_ref, dq_ref, dk_ref, gc_ref, bd_ref, g_ref,
                o_ref, acc_ref, st_ref):
    lane = lax.broadcasted_iota(jnp.int32, (1, GROUP_W), 1)
    hmask = [(lane >= h * RET_DK) & (lane < (h + 1) * RET_DK) for h in range(RET_HEADS)]
    ones_bd = bd_ref[...].astype(BF16)

    def qkv(r0):
        q = z_ref[0, pl.ds(r0, TM), 0:GROUP_W]
        k = z_ref[0, pl.ds(r0, TM), GROUP_W:2 * GROUP_W].astype(F32) * (RET_DK ** -0.5)
        v = z_ref[0, pl.ds(r0, TM), 2 * GROUP_W:3 * GROUP_W]
        return q, k, v

    def inner(q, k, v):
        kb = k.astype(BF16)
        acc = jnp.zeros((TM, GROUP_W), F32)
        for h in range(RET_HEADS):
            qh = jnp.where(hmask[h], q, jnp.zeros_like(q))
            vh = jnp.where(hmask[h], v, jnp.zeros_like(v))
            s = _bdot_t(qh, kb) * dm_ref[h]
            acc = acc + _bdot(s, vh)
        return acc

    def state_update(d, k, v):
        kd = (k * dk_ref[d]).T
        st_ref[...] = st_ref[...] * gc_ref[d] + _bdot(kd, v) * bd_ref[...]

    def cross(d, q):
        return _bdot(q.astype(F32) * dq_ref[d], st_ref[...])

    def finish(r0, o):
        sq = o * o
        hi = sq.astype(BF16)
        lo = (sq - hi.astype(F32)).astype(BF16)
        ms = (jnp.dot(hi, ones_bd, preferred_element_type=F32)
              + jnp.dot(lo, ones_bd, preferred_element_type=F32)) / RET_DK
        gate = z_ref[0, pl.ds(r0, TM), 3 * GROUP_W:4 * GROUP_W].astype(F32)
        y = o * lax.rsqrt(ms + EPS) * g_ref[...] * (gate * jax.nn.sigmoid(gate))
        o_ref[0, pl.ds(r0, TM), :] = y.astype(BF16)

    c0 = n_lat * TM
    qc, kc, vc = qkv(c0)
    if ctx_out:
        finish(c0, inner(qc, kc, vc))
    else:
        o_ref[0, pl.ds(c0, TM), :] = jnp.zeros((TM, GROUP_W), BF16)

    st_ref[...] = jnp.zeros_like(st_ref)
    state_update(0, kc, vc)

    def fwd(n, carry):
        r0 = pl.multiple_of(n * TM, TM)
        q, k, v = qkv(r0)
        acc_ref[pl.ds(r0, TM), :] = inner(q, k, v) + cross(0, q)
        state_update(0, k, v)
        return carry

    lax.fori_loop(0, n_lat, fwd, 0)

    st_ref[...] = jnp.zeros_like(st_ref)
    state_update(1, kc, vc)

    def bwd(i, carry):
        n = n_lat - 1 - i
        r0 = pl.multiple_of(n * TM, TM)
        q, k, v = qkv(r0)
        finish(r0, acc_ref[pl.ds(r0, TM), :] + cross(1, q))
        state_update(1, k, v)
        return carry

    lax.fori_loop(0, n_lat, bwd, 0)


def _retention(zr, tabs, g, n_batch, n_lat, ctx_out):
    seq = zr.shape[1]
    dm, dq, dk, gc, bd = tabs
    full = lambda a: pl.BlockSpec(a.shape, lambda b: (0,) * a.ndim)
    return pl.pallas_call(
        functools.partial(_ret_kernel, n_lat, ctx_out),
        out_shape=jax.ShapeDtypeStruct((n_batch, seq, GROUP_W), BF16),
        grid=(n_batch,),
        in_specs=[pl.BlockSpec((1, seq, Z_RET), lambda b: (b, 0, 0)),
                  full(dm), full(dq), full(dk), full(gc), full(bd), full(g)],
        out_specs=pl.BlockSpec((1, seq, GROUP_W), lambda b: (b, 0, 0)),
        scratch_shapes=[pltpu.VMEM((n_lat * TM, GROUP_W), F32),
                        pltpu.VMEM((GROUP_W, GROUP_W), F32)],
        compiler_params=_cparams(("parallel",)),
        name="retention",
    )(zr, dm, dq, dk, gc, bd, g)


def _retention_tables(lg):
    j = jnp.arange(TM, dtype=F32)
    rel = j[:, None] - j[None, :]
    lf = lg[0][:, None, None]
    lb = lg[1][:, None, None]
    dm = (jnp.where(rel[None] >= 0, jnp.exp(jnp.maximum(rel, 0.0)[None] * lf), 0.0)
          + jnp.where(rel[None] <= 0, jnp.exp(jnp.maximum(-rel, 0.0)[None] * lb), 0.0))
    lane_f = jnp.repeat(lg[0], RET_DK)[None, :]
    lane_b = jnp.repeat(lg[1], RET_DK)[None, :]
    dq = jnp.stack([jnp.exp((j + 1)[:, None] * lane_f), jnp.exp((TM - j)[:, None] * lane_b)])
    dk = jnp.stack([jnp.exp((TM - 1 - j)[:, None] * lane_f), jnp.exp(j[:, None] * lane_b)])
    gc = jnp.stack([jnp.exp(TM * lane_f).reshape(GROUP_W, 1), jnp.exp(TM * lane_b).reshape(GROUP_W, 1)])
    hid = jnp.arange(GROUP_W) // RET_DK
    bd = (hid[:, None] == hid[None, :]).astype(F32)
    return dm, dq, dk, gc, bd


def _hy_pre_kernel(n_lat, z_ref, w_ref, b_ref, s_ref, x0_ref):
    def seg(r0, n):
        z = z_ref[0, r0:r0 + n, :].astype(F32)
        u = (_shift_rows(z, -1) * w_ref[0:1, :] + z * w_ref[1:2, :]
             + _shift_rows(z, 1) * w_ref[2:3, :] + b_ref[...])
        x0_ref[0, r0:r0 + n, :] = u[:, 0:GROUP_W].astype(BF16)
        s_ref[0, r0:r0 + n, :] = (u[:, GROUP_W:2 * GROUP_W] * u[:, 2 * GROUP_W:]).astype(BF16)

    seg(0, n_lat * TM)
    seg(n_lat * TM, z_ref.shape[1] - n_lat * TM)


def _hy_pre(zh, w, b, n_batch, n_lat):
    seq = zh.shape[1]
    return pl.pallas_call(
        functools.partial(_hy_pre_kernel, n_lat),
        out_shape=[jax.ShapeDtypeStruct((n_batch, seq, GROUP_W), BF16)] * 2,
        grid=(n_batch,),
        in_specs=[pl.BlockSpec((1, seq, Z_HY), lambda i: (i, 0, 0)),
                  pl.BlockSpec(w.shape, lambda i: (0, 0)),
                  pl.BlockSpec(b.shape, lambda i: (0, 0))],
        out_specs=[pl.BlockSpec((1, seq, GROUP_W), lambda i: (i, 0, 0))] * 2,
        compiler_params=_cparams(("parallel",)),
        name="hyena_pre",
    )(zh, w, b)


def _hy_filter_kernel(z_ref, win_ref, w1_ref, b1_ref, w2_ref, b2_ref, w3_ref, fr_ref, hf_ref, hb_ref):
    fr = fr_ref[...]
    h = jnp.sin(fr * (_fdot(z_ref[...], w1_ref[...]) + b1_ref[...]))
    h = jnp.sin(fr * (_fdot(h, w2_ref[...]) + b2_ref[...]))
    filt = _fdot(h, w3_ref[...])
    win = win_ref[...]
    hf_ref[...] = filt[:, 0:GROUP_W] * win
    hb_ref[...] = filt[:, GROUP_W:] * win


def _hy_filter(n, w1p, b1, w2, b2, w3, freq):
    t = jnp.linspace(0.0, 1.0, n, dtype=F32)[:, None]
    bands = jnp.linspace(1e-4, HY_BANDS - 1, HY_BANDS, dtype=F32)
    w = 2.0 * math.pi * jnp.arange(n, dtype=F32)[:, None] / n
    z = jnp.concatenate([t, jnp.cos(bands * w), -jnp.sin(bands * w)], axis=-1)
    z = jnp.pad(z, ((0, 0), (0, LANES - z.shape[1])))
    max_decay = math.log(HY_TARGET) / HY_FAST_PCT
    min_decay = math.log(HY_TARGET) / HY_SLOW_PCT
    deltas = jnp.abs(jnp.linspace(min_decay, max_decay, GROUP_W, dtype=F32))
    window = jnp.exp(-t * deltas)
    full = lambda a: pl.BlockSpec(a.shape, lambda i: (0, 0))
    rows = lambda wd: pl.BlockSpec((TM, wd), lambda i: (i, 0))
    return pl.pallas_call(
        _hy_filter_kernel,
        out_shape=[jax.ShapeDtypeStruct((n, GROUP_W), F32)] * 2,
        grid=(n // TM,),
        in_specs=[rows(LANES), rows(GROUP_W), full(w1p), full(b1), full(w2), full(b2), full(w3),
                  full(freq)],
        out_specs=[rows(GROUP_W)] * 2,
        compiler_params=_cparams(("arbitrary",)),
        name="hyena_filter",
    )(z, window, w1p, b1, w2, b2, w3, freq)


def _toeplitz_rows(hf, hb, nj):
    n = hf.shape[0]
    full = jnp.concatenate([hf, jnp.zeros((1, GROUP_W), F32), hb[1:][::-1]], axis=0)
    d = jnp.arange(-(nj - 1), nj)[:, None]
    m = jnp.arange(2 * TM)[None, :]
    lag = d * TM + jnp.where(m < TM, m, m - 2 * TM)
    idx = jnp.mod(lag, 2 * n)
    return jnp.transpose(full[idx], (2, 0, 1))


def _hy_conv_kernel(nj, nb, s_ref, w_ref, y_ref, acc_ref):
    def chan(c, carry):
        sc = s_ref[c].reshape(nj * nb, TM)
        acc_ref[...] = jnp.zeros_like(acc_ref)
        for di in range(2 * nj - 1):
            d = di - (nj - 1)
            w = w_ref[c, di:di + 1, :]
            rolled = pltpu.roll(jnp.broadcast_to(w, (TM, 2 * TM)), 0, 1, stride=1, stride_axis=0)
            toep = rolled[:, 0:TM].astype(BF16)
            if d >= 0:
                res = jnp.dot(sc[0:(nj - d) * nb], toep, preferred_element_type=F32)
                acc_ref[d * nb:nj * nb, :] += res
            else:
                res = jnp.dot(sc[(-d) * nb:nj * nb], toep, preferred_element_type=F32)
                acc_ref[0:(nj + d) * nb, :] += res
        y_ref[c] = acc_ref[...].reshape(nj, nb, TM).astype(BF16)
        return carry

    lax.fori_loop(0, HY_CB, chan, 0)


def _hy_conv(s_t, w_rows):
    ch, nj, nb, _ = s_t.shape
    nd = 2 * nj - 1
    return pl.pallas_call(
        functools.partial(_hy_conv_kernel, nj, nb),
        out_shape=jax.ShapeDtypeStruct(s_t.shape, BF16),
        grid=(ch // HY_CB,),
        in_specs=[pl.BlockSpec((HY_CB, nj, nb, TM), lambda i: (i, 0, 0, 0)),
                  pl.BlockSpec((HY_CB, nd, 2 * TM), lambda i: (i, 0, 0))],
        out_specs=pl.BlockSpec((HY_CB, nj, nb, TM), lambda i: (i, 0, 0, 0)),
        scratch_shapes=[pltpu.VMEM((nj * nb, TM), F32)],
        compiler_params=_cparams(("parallel",)),
        name="hyena_conv",
    )(s_t, w_rows)


def _lru_tile_scan(a, b, carry, reverse):
    row = lax.broadcasted_iota(jnp.int32, (SUBLANES, 1), 0)
    for k in (1, 2, 4):
        if reverse:
            ok = row < SUBLANES - k
            sh = SUBLANES - k
        else:
            ok = row >= k
            sh = k
        a_s = jnp.where(ok, pltpu.roll(a, sh, 0), 1.0)
        b_s = jnp.where(ok, pltpu.roll(b, sh, 0), 0.0)
        b = a * b_s + b
        a = a * a_s
    return b + a * carry


def _lru_kernel(n_lat, z_ref, cw_ref, cb_ref, wg_ref, bg_ref, sp_ref, o_ref,
                af_ref, bf_ref, ab_ref, bb_ref, h_ref):
    seq = z_ref.shape[1]
    s_lat = n_lat * TM

    def prep(r0, n):
        x = z_ref[0, r0:r0 + n, 0:GROUP_W].astype(F32)
        x = (_shift_rows(x, -2) * cw_ref[0:1, :] + _shift_rows(x, -1) * cw_ref[1:2, :]
             + x * cw_ref[2:3, :] + _shift_rows(x, 1) * cw_ref[3:4, :] + cb_ref[...])
        g = _bdot(x, wg_ref[...]) + bg_ref[...]
        for d, (a_ref, b_ref) in enumerate(((af_ref, bf_ref), (ab_ref, bb_ref))):
            r = jax.nn.sigmoid(g[:, (2 * d) * GROUP_W:(2 * d + 1) * GROUP_W])
            i = jax.nn.sigmoid(g[:, (2 * d + 1) * GROUP_W:(2 * d + 2) * GROUP_W])
            log_a = -LRU_C * r * sp_ref[d:d + 1, :]
            a_ref[r0:r0 + n, :] = jnp.exp(log_a)
            b_ref[r0:r0 + n, :] = jnp.sqrt(1.0 - jnp.exp(2.0 * log_a)) * (i * x)

    prep(0, s_lat)
    prep(s_lat, seq - s_lat)

    n_ctx_tiles = (seq - s_lat) // SUBLANES
    n_lat_tiles = s_lat // SUBLANES

    def fwd_tile(base):
        def body(i, carry):
            r0 = pl.multiple_of(base + i * SUBLANES, SUBLANES)
            h = _lru_tile_scan(af_ref[pl.ds(r0, SUBLANES), :], bf_ref[pl.ds(r0, SUBLANES), :],
                               carry, False)
            h_ref[pl.ds(r0, SUBLANES), :] = h
            return h[SUBLANES - 1:SUBLANES, :]
        return body

    carry = jnp.zeros((1, GROUP_W), F32)
    carry = lax.fori_loop(0, n_ctx_tiles, fwd_tile(s_lat), carry)
    lax.fori_loop(0, n_lat_tiles, fwd_tile(0), carry)

    def bwd_tile(base, n_tiles):
        def body(i, carry):
            r0 = pl.multiple_of(base + (n_tiles - 1 - i) * SUBLANES, SUBLANES)
            h = _lru_tile_scan(ab_ref[pl.ds(r0, SUBLANES), :], bb_ref[pl.ds(r0, SUBLANES), :],
                               carry, True)
            gate = z_ref[0, pl.ds(r0, SUBLANES), GROUP_W:2 * GROUP_W].astype(F32)
            y = (h + h_ref[pl.ds(r0, SUBLANES), :]) * jax.nn.gelu(gate, approximate=True)
            o_ref[0, pl.ds(r0, SUBLANES), :] = y.astype(BF16)
            return h[0:1, :]
        return body

    carry = jnp.zeros((1, GROUP_W), F32)
    carry = lax.fori_loop(0, n_ctx_tiles, bwd_tile(s_lat, n_ctx_tiles), carry)
    lax.fori_loop(0, n_lat_tiles, bwd_tile(0, n_lat_tiles), carry)


def _lru(zl, cw, cb, wg, bg, sp, n_batch, n_lat):
    seq = zl.shape[1]
    full = lambda a: pl.BlockSpec(a.shape, lambda i: (0,) * a.ndim)
    return pl.pallas_call(
        functools.partial(_lru_kernel, n_lat),
        out_shape=jax.ShapeDtypeStruct((n_batch, seq, GROUP_W), BF16),
        grid=(n_batch,),
        in_specs=[pl.BlockSpec((1, seq, Z_LRU), lambda i: (i, 0, 0)),
                  full(cw), full(cb), full(wg), full(bg), full(sp)],
        out_specs=pl.BlockSpec((1, seq, GROUP_W), lambda i: (i, 0, 0)),
        scratch_shapes=[pltpu.VMEM((seq, GROUP_W), F32)] * 5,
        compiler_params=_cparams(("parallel",)),
        name="rglru",
    )(zl, cw, cb, wg, bg, sp)


def _merge_kernel(x_ref, ya_ref, yb_ref, yc_ref, s_ref, x0_ref, yd_ref, hd_ref, gg_ref, wo_ref,
                  mod_ref, g2_ref, wr_ref, x1_ref, f_ref, rt_ref, cat_ref):
    s = s_ref[0].astype(F32)
    yh = x0_ref[0].astype(F32) * (yc_ref[0].astype(F32) + s * hd_ref[...])
    parts = (ya_ref[0].astype(F32), yb_ref[0].astype(F32), yh, yd_ref[0].astype(F32))
    for g, y in enumerate(parts):
        ms = jnp.mean(y * y, axis=-1, keepdims=True)
        gg = gg_ref[:, g * GROUP_W:(g + 1) * GROUP_W]
        cat_ref[:, g * GROUP_W:(g + 1) * GROUP_W] = (y * lax.rsqrt(ms + EPS) * gg).astype(BF16)
    m = jnp.dot(cat_ref[...], wo_ref[...], preferred_element_type=F32)
    gate1 = mod_ref[0, :, 2 * D_MODEL:3 * D_MODEL]
    x1 = x_ref[...] + gate1 * m
    x1_ref[...] = x1
    ms = jnp.mean(x1 * x1, axis=-1, keepdims=True)
    sh = mod_ref[0, :, 3 * D_MODEL:4 * D_MODEL]
    sc = mod_ref[0, :, 4 * D_MODEL:5 * D_MODEL]
    f = (x1 * lax.rsqrt(ms + EPS) * g2_ref[...]) * (1.0 + sc) + sh
    fb = f.astype(BF16)
    f_ref[...] = fb
    rt_ref[...] = _route(jnp.dot(fb, wr_ref[...], preferred_element_type=F32))


def _route(logits):
    lane = lax.broadcasted_iota(jnp.int32, logits.shape, 1)
    neg = jnp.float32(-jnp.inf)
    big = jnp.int32(LANES)
    gl = jnp.where(lane < MOE_GROUPS, logits, neg)
    gmax = jnp.max(gl, axis=-1, keepdims=True)
    gsum = jnp.sum(jnp.exp(gl - gmax), axis=-1, keepdims=True)
    g_val = 1.0 / gsum
    g_idx = jnp.min(jnp.where(gl == gmax, lane, big), axis=-1, keepdims=True)
    lo = MOE_GROUPS + MOE_PER_GROUP * g_idx
    el = jnp.where((lane >= lo) & (lane < lo + MOE_PER_GROUP), logits, neg)
    m1 = jnp.max(el, axis=-1, keepdims=True)
    i1 = jnp.min(jnp.where(el == m1, lane, big), axis=-1, keepdims=True)
    el2 = jnp.where(lane == i1, neg, el)
    m2 = jnp.max(el2, axis=-1, keepdims=True)
    i2 = jnp.min(jnp.where(el2 == m2, lane, big), axis=-1, keepdims=True)
    r = jnp.exp(m2 - m1)
    w1 = g_val / (1.0 + r)
    w2 = w1 * r
    e1 = (i1 - MOE_GROUPS).astype(F32)
    e2 = (i2 - MOE_GROUPS).astype(F32)
    return jnp.where(lane == 0, e1, jnp.where(lane == 1, e2, jnp.where(lane == 2, w1,
                     jnp.where(lane == 3, w2, 0.0))))


def _merge(xx, ys, hd, gg, wo, mod, g2, wr, n_batch, nt_all, nt, n_lat):
    rows_all = xx.shape[0]
    rows_out = n_batch * nt * TM
    in_rows = lambda b, t: (b * nt_all + t, 0)
    out_rows = lambda b, t: (b * nt + t, 0)
    seq_blk = pl.BlockSpec((1, TM, GROUP_W), lambda b, t: (b, t, 0))
    small = lambda a: pl.BlockSpec(a.shape, lambda b, t: (0, 0))
    return pl.pallas_call(
        _merge_kernel,
        out_shape=[jax.ShapeDtypeStruct((rows_all, D_MODEL), F32),
                   jax.ShapeDtypeStruct((rows_out, D_MODEL), BF16),
                   jax.ShapeDtypeStruct((rows_out, LANES), F32)],
        grid=(n_batch, nt),
        in_specs=[pl.BlockSpec((TM, D_MODEL), in_rows)] + [seq_blk] * 6
                 + [small(hd), small(gg), small(wo),
                    pl.BlockSpec((1, 1, 6 * D_MODEL), _mod_index(nt_all, n_lat, n_batch)),
                    small(g2), small(wr)],
        out_specs=[pl.BlockSpec((TM, D_MODEL), in_rows),
                   pl.BlockSpec((TM, D_MODEL), out_rows),
                   pl.BlockSpec((TM, LANES), out_rows)],
        scratch_shapes=[pltpu.VMEM((TM, D_MODEL), BF16)],
        input_output_aliases={0: 0},
        compiler_params=_cparams(("parallel", "arbitrary")),
        name="merge_outproj",
    )(xx, *ys, hd, gg, wo, mod, g2, wr)


def _expert_kernel(be_ref, nv_ref, x_ref, w1_ref, w3_ref, w2_ref, o_ref, w1b_ref, w3b_ref, w2b_ref):
    i = pl.program_id(0)

    @pl.when((i == 0) | (be_ref[i] != be_ref[jnp.maximum(i - 1, 0)]))
    def _():
        w1b_ref[...] = w1_ref[0, 0].astype(BF16)
        w3b_ref[...] = w3_ref[0, 0].astype(BF16)
        w2b_ref[...] = w2_ref[0, 0].astype(BF16)

    @pl.when(i < nv_ref[0])
    def _():
        x = x_ref[...]
        a = jnp.dot(x, w1b_ref[...], preferred_element_type=F32)
        b = jnp.dot(x, w3b_ref[...], preferred_element_type=F32)
        h = (a * jax.nn.sigmoid(a)) * b
        o_ref[...] = jnp.dot(h.astype(BF16), w2b_ref[...], preferred_element_type=F32).astype(BF16)

    @pl.when(i >= nv_ref[0])
    def _():
        o_ref[...] = jnp.zeros_like(o_ref)


def _experts(xs, block_exp, n_valid, w1, w3, w2, layer):
    n_slots = xs.shape[0]
    n_blocks = n_slots // MOE_BM
    w_idx = lambda i, be, nv: (layer, be[i], 0, 0)
    grid_spec = pltpu.PrefetchScalarGridSpec(
        num_scalar_prefetch=2,
        grid=(n_blocks,),
        in_specs=[pl.BlockSpec((MOE_BM, D_MODEL), lambda i, be, nv: (i, 0)),
                  pl.BlockSpec((1, 1, D_MODEL, EXPERT_FF), w_idx),
                  pl.BlockSpec((1, 1, D_MODEL, EXPERT_FF), w_idx),
                  pl.BlockSpec((1, 1, EXPERT_FF, D_MODEL), w_idx)],
        out_specs=pl.BlockSpec((MOE_BM, D_MODEL), lambda i, be, nv: (i, 0)),
        scratch_shapes=[pltpu.VMEM((D_MODEL, EXPERT_FF), BF16),
                        pltpu.VMEM((D_MODEL, EXPERT_FF), BF16),
                        pltpu.VMEM((EXPERT_FF, D_MODEL), BF16)],
    )
    return pl.pallas_call(
        _expert_kernel,
        out_shape=jax.ShapeDtypeStruct((n_slots, D_MODEL), BF16),
        grid_spec=grid_spec,
        compiler_params=_cparams(("arbitrary",)),
        name="moe_experts",
    )(block_exp, n_valid, xs, w1, w3, w2)


def _combine_kernel(x_ref, ya_ref, yb_ref, rt_ref, mod_ref, o_ref):
    gate2 = mod_ref[0, :, 5 * D_MODEL:6 * D_MODEL]
    wa = rt_ref[:, 2:3]
    wb = rt_ref[:, 3:4]
    o_ref[...] = x_ref[...] + gate2 * (wa * ya_ref[...].astype(F32) + wb * yb_ref[...].astype(F32))


def _combine(xx, ya, yb, route, mod, n_batch, nt_all, nt, n_lat, in_place):
    in_rows = lambda b, t: (b * nt_all + t, 0)
    out_rows = lambda b, t: (b * nt + t, 0)
    o_rows = in_rows if in_place else out_rows
    rows = xx.shape[0] if in_place else n_batch * nt * TM
    return pl.pallas_call(
        _combine_kernel,
        out_shape=jax.ShapeDtypeStruct((rows, D_MODEL), F32),
        grid=(n_batch, nt),
        in_specs=[pl.BlockSpec((TM, D_MODEL), in_rows),
                  pl.BlockSpec((TM, D_MODEL), out_rows),
                  pl.BlockSpec((TM, D_MODEL), out_rows),
                  pl.BlockSpec((TM, LANES), out_rows),
                  pl.BlockSpec((1, 1, 6 * D_MODEL), _mod_index(nt_all, n_lat, n_batch))],
        out_specs=pl.BlockSpec((TM, D_MODEL), o_rows),
        input_output_aliases={0: 0} if in_place else {},
        compiler_params=_cparams(("parallel", "arbitrary")),
        name="moe_combine",
    )(xx, ya, yb, route, mod)


def _dispatch_plan(eid):
    t = eid.shape[0]
    n_assign = t * TOP_K
    e_flat = eid.reshape(-1)
    experts = jnp.arange(N_EXPERTS, dtype=jnp.int32)
    e_sorted, a_sorted = lax.sort((e_flat, jnp.arange(n_assign, dtype=jnp.int32)), num_keys=1,
                                  is_stable=True)
    counts = jnp.sum((e_flat[:, None] == experts[None, :]).astype(jnp.int32), axis=0)
    padded = (counts + MOE_BM - 1) // MOE_BM * MOE_BM
    pad_end = jnp.cumsum(padded)
    shift = (pad_end - padded) - (jnp.cumsum(counts) - counts)
    n_blocks = -(-n_assign // MOE_BM) + N_EXPERTS
    n_slots = n_blocks * MOE_BM
    blk0 = jnp.arange(n_blocks, dtype=jnp.int32) * MOE_BM
    block_exp = jnp.minimum(jnp.sum((pad_end[None, :] <= blk0[:, None]).astype(jnp.int32), axis=1),
                            N_EXPERTS - 1)
    n_valid = (pad_end[-1] // MOE_BM).astype(jnp.int32).reshape(1)
    block_shift = jnp.sum(jnp.where(block_exp[:, None] == experts[None, :], shift[None, :], 0), axis=1)
    src = jnp.arange(n_slots, dtype=jnp.int32) - jnp.repeat(block_shift, MOE_BM)
    slot_tok = a_sorted.at[jnp.clip(src, 0, n_assign - 1)].get(mode='promise_in_bounds') // TOP_K
    dest = (jnp.arange(n_assign, dtype=jnp.int32)
            + jnp.sum(jnp.where(e_sorted[:, None] == experts[None, :], shift[None, :], 0), axis=1))
    _, pos = lax.sort((a_sorted, dest), num_keys=1)
    return slot_tok, pos.reshape(t, TOP_K), block_exp, n_valid


def _take_rows(a, idx):
    return a.at[idx].get(mode='promise_in_bounds')


def _moe(f, route, w1, w3, w2, layer):
    eid = route[:, 0:TOP_K].astype(jnp.int32)
    slot_tok, pos, block_exp, n_valid = _dispatch_plan(eid)
    ys = _experts(_take_rows(f, slot_tok), block_exp, n_valid, w1, w3, w2, layer)
    return _take_rows(ys, pos[:, 0]), _take_rows(ys, pos[:, 1])


def _pack_w_in(w_in):
    o = 0
    cols = {}
    for name, wd in (('cq', 192), ('ckv', 128), ('kr', 32), ('ret', 1024), ('hy', 768), ('lru', 512)):
        cols[name] = w_in[:, o:o + wd]
        o += wd
    swap = jnp.arange(MLA_ROPE) ^ 8
    kr_sw = cols['kr'][:, swap]
    return jnp.concatenate([cols['cq'], cols['kr'], kr_sw, cols['ckv'], cols['ret'], cols['hy'],
                            cols['lru']], axis=1).astype(BF16)


def _pack_mla(w_uq, w_ukv, q_norm_g, kv_norm_g, qn_g, kn_g):
    hw = MLA_HEADS * LANES
    swap = jnp.arange(MLA_ROPE) ^ 8
    wq = jnp.zeros((2 * LANES, 2 * hw), F32)
    wkv = jnp.zeros((LANES, hw + MLA_HEADS * GROUP_W), F32)
    for h in range(MLA_HEADS):
        qh = w_uq[:, h * MLA_QK:(h + 1) * MLA_QK]
        wq = wq.at[:MLA_Q_RANK, h * LANES:h * LANES + MLA_QK].set(qh)
        wq = wq.at[:MLA_Q_RANK, hw + h * LANES + MLA_NOPE:hw + h * LANES + MLA_QK].set(
            qh[:, MLA_NOPE + swap])
        kvh = w_ukv[:, h * (MLA_NOPE + MLA_V):(h + 1) * (MLA_NOPE + MLA_V)]
        wkv = wkv.at[:, h * LANES:h * LANES + MLA_NOPE].set(kvh[:, :MLA_NOPE])
        v0 = hw + h * GROUP_W + h * MLA_V
        wkv = wkv.at[:, v0:v0 + MLA_V].set(kvh[:, MLA_NOPE:])
    pad = lambda v, lo, n: jnp.zeros((1, n), F32).at[0, lo:lo + v.shape[0]].set(v)
    gains = (pad(q_norm_g, 0, 2 * LANES), kv_norm_g.reshape(1, LANES),
             pad(qn_g, 0, LANES), pad(qn_g[MLA_NOPE + swap], MLA_NOPE, LANES),
             pad(kn_g[:MLA_NOPE], 0, LANES), pad(kn_g[MLA_NOPE:], MLA_NOPE, LANES),
             pad(kn_g[MLA_NOPE + swap], MLA_NOPE, LANES))
    return wq.astype(BF16), wkv.astype(BF16), gains


def _rope_tables(s_lat, seq):
    rows = s_lat // GRID_W
    row = jnp.repeat(jnp.arange(rows), GRID_W).astype(F32)
    col = jnp.tile(jnp.arange(GRID_W), rows).astype(F32)
    half = MLA_ROPE // 4
    inv_freq = ROPE_BASE ** (-jnp.arange(half, dtype=F32) / half)
    ar = row[:, None] * inv_freq
    ac = col[:, None] * inv_freq
    cos32 = jnp.concatenate([jnp.cos(ar), jnp.cos(ar), jnp.cos(ac), jnp.cos(ac)], axis=1)
    sin32 = jnp.concatenate([-jnp.sin(ar), jnp.sin(ar), -jnp.sin(ac), jnp.sin(ac)], axis=1)
    cos_t = jnp.ones((seq, LANES), F32).at[:s_lat, MLA_NOPE:MLA_QK].set(cos32)
    sin_t = jnp.zeros((seq, LANES), F32).at[:s_lat, MLA_NOPE:MLA_QK].set(sin32)
    return cos_t, sin_t


def _block_diag(w):
    nb, bw, _ = w.shape
    out = jnp.zeros((nb * bw, nb * bw), F32)
    for i in range(nb):
        out = out.at[i * bw:(i + 1) * bw, i * bw:(i + 1) * bw].set(w[i])
    return out


def kernel(x, c, ctx, c_ctx, w_mod, b_mod, norm1_g, norm2_g, w_in, mla_q_norm_g, mla_w_uq, mla_kv_norm_g, mla_w_ukv, mla_qn_g, mla_kn_g, ret_log_gamma, ret_norm_g, hy_conv_w, hy_conv_b, hy_w1, hy_b1, hy_w2, hy_b2, hy_w3, hy_freq, hy_d, lru_conv_w, lru_conv_b, lru_wa, lru_ba, lru_wx, lru_bx, lru_lambda, group_norm_g, w_out, moe_w_group, moe_w_expert, moe_w1, moe_w3, moe_w2):
    n_batch, s_lat, d = x.shape
    n_ctx = ctx.shape[1]
    depth = w_mod.shape[0]
    assert d == D_MODEL and n_ctx == TM and s_lat % TM == 0 and s_lat % GRID_W == 0
    n_lat = s_lat // TM
    nt_all = n_lat + 1
    seq = nt_all * TM
    mod_rows = -(-(n_batch + 1) // SUBLANES) * SUBLANES

    cc = jnp.zeros((mod_rows, d), F32).at[:n_batch].set(c).at[n_batch].set(c_ctx)
    mod_all = _modulation(cc, w_mod, b_mod)
    xx = jnp.concatenate([x, ctx], axis=1).reshape(n_batch * seq, d)
    cos_t, sin_t = _rope_tables(s_lat, seq)

    for l in range(depth):
        ctx_out = l < depth - 1
        nt = nt_all if ctx_out else n_lat
        mod = mod_all[l].reshape(mod_rows, 1, 6 * d)

        zm, zr, zh, zl = _inproj(xx, mod, norm1_g[l].reshape(1, d), _pack_w_in(w_in[l]),
                                 n_batch, nt_all, n_lat)
        zm = zm.reshape(n_batch, seq, Z_MLA)
        zr = zr.reshape(n_batch, seq, Z_RET)
        zh = zh.reshape(n_batch, seq, Z_HY)
        zl = zl.reshape(n_batch, seq, Z_LRU)

        wq, wkv, gains = _pack_mla(mla_w_uq[l], mla_w_ukv[l], mla_q_norm_g[l], mla_kv_norm_g[l],
                                   mla_qn_g[l], mla_kn_g[l])
        q, k, v = _mla_prep(zm, cos_t, sin_t, gains, wq, wkv, n_batch, nt_all)
        y_mla = _attention(q, k, v, n_batch, nt, n_lat)

        y_ret = _retention(zr, _retention_tables(ret_log_gamma[l].astype(F32)),
                           ret_norm_g[l].reshape(1, GROUP_W), n_batch, n_lat, ctx_out)

        s_hy, x0_hy = _hy_pre(zh, hy_conv_w[l], hy_conv_b[l].reshape(1, Z_HY), n_batch, n_lat)
        w1p = jnp.pad(hy_w1[l], ((0, LANES - hy_w1.shape[1]), (0, 0)))
        fargs = (w1p, hy_b1[l].reshape(1, -1), hy_w2[l], hy_b2[l].reshape(1, -1), hy_w3[l],
                 hy_freq[l].reshape(1, -1))
        hf, hb = _hy_filter(s_lat, *fargs)
        s_t = jnp.transpose(s_hy[:, :s_lat].reshape(n_batch, n_lat, TM, GROUP_W), (3, 1, 0, 2))
        y_t = _hy_conv(s_t, _toeplitz_rows(hf, hb, n_lat))
        y_hy = jnp.transpose(y_t, (2, 1, 3, 0)).reshape(n_batch, s_lat, GROUP_W)
        if ctx_out:
            hf_c, hb_c = _hy_filter(n_ctx, *fargs)
            sc_t = jnp.transpose(s_hy[:, s_lat:].reshape(n_batch, 1, TM, GROUP_W), (3, 1, 0, 2))
            yc_t = _hy_conv(sc_t, _toeplitz_rows(hf_c, hb_c, 1))
            y_hy_c = jnp.transpose(yc_t, (2, 1, 3, 0)).reshape(n_batch, n_ctx, GROUP_W)
        else:
            y_hy_c = jnp.zeros((n_batch, n_ctx, GROUP_W), BF16)
        y_hy = jnp.concatenate([y_hy, y_hy_c], axis=1)

        wg = jnp.concatenate([_block_diag(lru_wa[l, 0]), _block_diag(lru_wx[l, 0]),
                              _block_diag(lru_wa[l, 1]), _block_diag(lru_wx[l, 1])], axis=1).astype(BF16)
        bg = jnp.concatenate([lru_ba[l, 0], lru_bx[l, 0], lru_ba[l, 1], lru_bx[l, 1]]).reshape(1, -1)
        sp = jax.nn.softplus(-lru_lambda[l].astype(F32))
        y_lru = _lru(zl, lru_conv_w[l], lru_conv_b[l].reshape(1, -1), wg, bg, sp, n_batch, n_lat)

        wr = jnp.zeros((d, LANES), F32).at[:, :MOE_GROUPS].set(moe_w_group[l])
        wr = wr.at[:, MOE_GROUPS:MOE_GROUPS + N_EXPERTS].set(moe_w_expert[l]).astype(BF16)
        xx, f, route = _merge(xx, (y_mla, y_ret, y_hy, s_hy, x0_hy, y_lru), hy_d[l].reshape(1, -1),
                               group_norm_g[l].reshape(1, -1), w_out[l].astype(BF16), mod,
                               norm2_g[l].reshape(1, d), wr, n_batch, nt_all, nt, n_lat)

        ya, yb = _moe(f, route, moe_w1, moe_w3, moe_w2, l)
        xx = _combine(xx, ya, yb, route, mod, n_batch, nt_all, nt, n_lat, in_place=ctx_out)

    return xx.reshape(n_batch, s_lat, d)
```

```python
import functools
import math

import jax
import jax.numpy as jnp
from jax import lax
from jax.experimental import pallas as pl
from jax.experimental.pallas import tpu as pltpu

F32 = jnp.float32
BF16 = jnp.bfloat16

D_MODEL = 1024
EPS = 1e-6
GRID_W = 64
N_GROUPS = 4
GROUP_W = 256

MLA_HEADS = 4
MLA_NOPE = 64
MLA_ROPE = 32
MLA_QK = 96
MLA_V = 64
MLA_Q_RANK = 192
MLA_KV_RANK = 128
ROPE_BASE = 10000.0

RET_HEADS = 4
RET_DK = 64

HY_BANDS = 16
HY_FAST_PCT = 0.3
HY_SLOW_PCT = 1.5
HY_TARGET = 1e-2

LRU_C = 8.0

MOE_GROUPS = 4
MOE_PER_GROUP = 8
N_EXPERTS = 32
TOP_K = 2
EXPERT_FF = 512

LANES = 128
SUBLANES = 8
MXU_DIM = 256

TM = 256
MOE_BM = 512
HY_CB = 8
SCAN_UNROLL = 8
RET_UNROLL = 4
VMEM_LIMIT = 56 * 1024 * 1024

Z_MLA, Z_RET, Z_HY, Z_LRU = 384, 1024, 768, 512
Z_COLS = Z_MLA + Z_RET + Z_HY + Z_LRU


def _cparams(sem):
    return pltpu.CompilerParams(dimension_semantics=sem, vmem_limit_bytes=VMEM_LIMIT)


def _bdot(a, b):
    return jnp.dot(a.astype(BF16), b.astype(BF16), preferred_element_type=F32)


def _bdot_t(a, b):
    return lax.dot_general(a.astype(BF16), b.astype(BF16), (((1,), (1,)), ((), ())),
                           preferred_element_type=F32)


def _fdot(a, b):
    return jnp.dot(a, b, preferred_element_type=F32, precision=lax.Precision.HIGHEST)


def _sigmoid(x):
    return 0.5 * jnp.tanh(0.5 * x) + 0.5


def _shift_rows(x, d):
    n = x.shape[0]
    r = pltpu.roll(x, (n - d) % n, 0)
    row = lax.broadcasted_iota(jnp.int32, (n, 1), 0)
    ok = (row + d >= 0) & (row + d < n)
    return jnp.where(ok, r, 0.0)


def _mod_kernel(c_ref, w_ref, b_ref, o_ref):
    c = c_ref[...]
    a = c * jax.nn.sigmoid(c)
    o_ref[0] = _fdot(a, w_ref[0]) + b_ref[0]


def _modulation(cc, w_mod, b_mod):
    nl, d, n6 = w_mod.shape
    rows = cc.shape[0]
    tn = 1536
    return pl.pallas_call(
        _mod_kernel,
        out_shape=jax.ShapeDtypeStruct((nl, rows, n6), F32),
        grid=(nl, n6 // tn),
        in_specs=[pl.BlockSpec((rows, d), lambda l, j: (0, 0)),
                  pl.BlockSpec((1, d, tn), lambda l, j: (l, 0, j)),
                  pl.BlockSpec((1, 1, tn), lambda l, j: (l, 0, j))],
        out_specs=pl.BlockSpec((1, rows, tn), lambda l, j: (l, 0, j)),
        compiler_params=_cparams(("arbitrary", "arbitrary")),
        name="modulation",
    )(cc, w_mod, b_mod.reshape(nl, 1, n6))


def _inproj_kernel(x_ref, mod_ref, g_ref, w_ref, zm_ref, zr_ref, zh_ref, zl_ref):
    x = x_ref[...]
    ms = jnp.mean(x * x, axis=-1, keepdims=True)
    y = x * lax.rsqrt(ms + EPS) * g_ref[...]
    sh = mod_ref[0, :, 0:D_MODEL]
    sc = mod_ref[0, :, D_MODEL:2 * D_MODEL]
    a = y * (1.0 + sc) + sh
    z = _bdot(a, w_ref[...])
    o = 0
    for ref, w in ((zm_ref, Z_MLA), (zr_ref, Z_RET), (zh_ref, Z_HY), (zl_ref, Z_LRU)):
        ref[...] = z[:, o:o + w].astype(BF16)
        o += w


def _mod_index(nt_all, n_lat, n_batch):
    def idx(b, t):
        return (jnp.where(t < n_lat, b, n_batch), 0, 0)
    return idx


def _inproj(xx, mod, g1, w_in_p, n_batch, nt, n_lat):
    rows = xx.shape[0]
    row_map = lambda b, t: (b * nt + t, 0)
    outs = [jax.ShapeDtypeStruct((rows, w), BF16) for w in (Z_MLA, Z_RET, Z_HY, Z_LRU)]
    return pl.pallas_call(
        _inproj_kernel,
        out_shape=outs,
        grid=(n_batch, nt),
        in_specs=[pl.BlockSpec((TM, D_MODEL), row_map),
                  pl.BlockSpec((1, 1, 6 * D_MODEL), _mod_index(nt, n_lat, n_batch)),
                  pl.BlockSpec((1, D_MODEL), lambda b, t: (0, 0)),
                  pl.BlockSpec((D_MODEL, Z_COLS), lambda b, t: (0, 0))],
        out_specs=[pl.BlockSpec((TM, w), row_map) for w in (Z_MLA, Z_RET, Z_HY, Z_LRU)],
        compiler_params=_cparams(("parallel", "arbitrary")),
        name="inproj",
    )(xx, mod, g1, w_in_p)


def _mla_kernel(n_lat, nt_all, z_ref, cos_ref, sin_ref, gq_ref, gkv_ref, gqm_ref, gqs_ref, gkm_ref,
                gkr_ref, gks_ref, wq_ref, wkv_ref, o_ref, k_scr, v_scr):
    t = pl.program_id(1)
    hw = MLA_HEADS * LANES
    lane = lax.broadcasted_iota(jnp.int32, (1, LANES), 1)
    is_rope = (lane >= MLA_NOPE) & (lane < MLA_QK)
    vlane = lax.broadcasted_iota(jnp.int32, (1, GROUP_W), 1)

    def kv_tile(i, carry):
        r0 = pl.multiple_of(i * TM, TM)
        z = z_ref[0, pl.ds(r0, TM), LANES:3 * LANES].astype(F32)
        col1 = z[:, 0:LANES]
        zb = z[:, LANES:2 * LANES]
        cos = cos_ref[pl.ds(r0, TM), :]
        sin = sin_ref[pl.ds(r0, TM), :]
        ms_kv = jnp.mean(zb * zb, axis=-1, keepdims=True)
        kv = _bdot(zb * lax.rsqrt(ms_kv + EPS) * gkv_ref[...], wkv_ref[...])
        kr = jnp.where(is_rope, col1, 0.0)
        kr_sw = jnp.where(is_rope, pltpu.roll(col1, LANES - MLA_ROPE, 1), 0.0)
        k_rot = kr * gkr_ref[...] * cos + kr_sw * gks_ref[...] * sin
        ss_kr = jnp.sum(kr * kr, axis=-1, keepdims=True)
        for h in range(MLA_HEADS):
            kn = kv[:, h * LANES:(h + 1) * LANES]
            rk = lax.rsqrt((jnp.sum(kn * kn, axis=-1, keepdims=True) + ss_kr) / MLA_QK + EPS)
            k_scr[pl.ds(r0, TM), h * LANES:(h + 1) * LANES] = (
                rk * (kn * gkm_ref[...] + k_rot)).astype(BF16)
            vh = kv[:, hw + h * GROUP_W:hw + (h + 1) * GROUP_W]
            ones_lane = ((h + 1) % MLA_HEADS) * MLA_V
            v_scr[pl.ds(r0, TM), h * GROUP_W:(h + 1) * GROUP_W] = jnp.where(
                vlane == ones_lane, 1.0, vh).astype(BF16)
        return carry

    @pl.when(t == 0)
    def _():
        lax.fori_loop(0, nt_all, kv_tile, 0)

    q0 = pl.multiple_of(t * TM, TM)
    za = z_ref[0, pl.ds(q0, TM), 0:2 * LANES].astype(F32)
    cos = cos_ref[pl.ds(q0, TM), :]
    sin = sin_ref[pl.ds(q0, TM), :]
    lane2 = lax.broadcasted_iota(jnp.int32, (1, 2 * LANES), 1)
    ms_q = jnp.sum(jnp.where(lane2 < MLA_Q_RANK, za * za, 0.0), axis=-1, keepdims=True) / MLA_Q_RANK
    qall = _bdot(za * lax.rsqrt(ms_q + EPS) * gq_ref[...], wq_ref[...])
    q_heads = []
    for h in range(MLA_HEADS):
        qh = qall[:, h * LANES:(h + 1) * LANES]
        qs = qall[:, hw + h * LANES:hw + (h + 1) * LANES]
        rs = lax.rsqrt(jnp.sum(qh * qh, axis=-1, keepdims=True) / MLA_QK + EPS) * (MLA_QK ** -0.5)
        q_heads.append((rs * (qh * gqm_ref[...] * cos + qs * gqs_ref[...] * sin)).astype(BF16))

    def attend(r0, n):
        acc = jnp.zeros((TM, GROUP_W), F32)
        for h in range(MLA_HEADS):
            s = _bdot_t(q_heads[h], k_scr[r0:r0 + n, h * LANES:(h + 1) * LANES])
            m = jnp.max(s, axis=-1, keepdims=True)
            p = jnp.exp((s - m).astype(BF16))
            pv = jnp.dot(p, v_scr[r0:r0 + n, h * GROUP_W:(h + 1) * GROUP_W],
                         preferred_element_type=F32)
            ones_lane = ((h + 1) % MLA_HEADS) * MLA_V
            l = jnp.sum(jnp.where(vlane == ones_lane, pv, 0.0), axis=-1, keepdims=True)
            in_head = (vlane >= h * MLA_V) & (vlane < (h + 1) * MLA_V)
            acc = acc + jnp.where(in_head, pv * (1.0 / l), 0.0)
        o_ref[0] = acc.astype(BF16)

    @pl.when(t < n_lat)
    def _():
        attend(0, nt_all * TM)

    @pl.when(t >= n_lat)
    def _():
        attend(n_lat * TM, (nt_all - n_lat) * TM)


def _mla(zm, cos_t, sin_t, gains, wq, wkv, n_batch, nt, n_lat):
    seq = zm.shape[1]
    nt_all = seq // TM
    hw = MLA_HEADS * LANES
    vw = MLA_HEADS * GROUP_W
    small = lambda w: pl.BlockSpec((1, w), lambda b, t: (0, 0))
    return pl.pallas_call(
        functools.partial(_mla_kernel, n_lat, nt_all),
        out_shape=jax.ShapeDtypeStruct((n_batch, nt * TM, GROUP_W), BF16),
        grid=(n_batch, nt),
        in_specs=[pl.BlockSpec((1, seq, Z_MLA), lambda b, t: (b, 0, 0)),
                  pl.BlockSpec((seq, LANES), lambda b, t: (0, 0)),
                  pl.BlockSpec((seq, LANES), lambda b, t: (0, 0)),
                  small(2 * LANES), small(LANES), small(LANES), small(LANES), small(LANES),
                  small(LANES), small(LANES),
                  pl.BlockSpec((2 * LANES, 2 * hw), lambda b, t: (0, 0)),
                  pl.BlockSpec((LANES, hw + vw), lambda b, t: (0, 0))],
        out_specs=pl.BlockSpec((1, TM, GROUP_W), lambda b, t: (b, t, 0)),
        scratch_shapes=[pltpu.VMEM((seq, hw), BF16), pltpu.VMEM((seq, vw), BF16)],
        compiler_params=_cparams(("parallel", "arbitrary")),
        name="mla_attention",
    )(zm, cos_t, sin_t, *gains, wq, wkv)


def _ret_kernel(n_lat, ctx_out, z_ref, dm_ref, dq_ref, dk_ref, gc_ref, bd_ref, g_ref,
                o_ref, acc_ref, st_ref):
    lane = lax.broadcasted_iota(jnp.int32, (1, GROUP_W), 1)
    hmask = [(lane >= h * RET_DK) & (lane < (h + 1) * RET_DK) for h in range(RET_HEADS)]
    ones_bd = bd_ref[...].astype(BF16)

    def qkv(r0):
        q = z_ref[0, pl.ds(r0, TM), 0:GROUP_W]
        k = z_ref[0, pl.ds(r0, TM), GROUP_W:2 * GROUP_W].astype(F32) * (RET_DK ** -0.5)
        v = z_ref[0, pl.ds(r0, TM), 2 * GROUP_W:3 * GROUP_W]
        return q, k, v

    def inner(q, k, v):
        kb = k.astype(BF16)
        acc = jnp.zeros((TM, GROUP_W), F32)
        for h in range(RET_HEADS):
            qh = jnp.where(hmask[h], q, jnp.zeros_like(q))
            vh = jnp.where(hmask[h], v, jnp.zeros_like(v))
            s = _bdot_t(qh, kb) * dm_ref[h]
            acc = acc + _bdot(s, vh)
        return acc

    def state_update(d, k, v):
        kd = (k * dk_ref[d]).T
        st_ref[...] = st_ref[...] * gc_ref[d] + _bdot(kd, v) * bd_ref[...]

    def cross(d, q):
        return _bdot(q.astype(F32) * dq_ref[d], st_ref[...])

    def finish(r0, o):
        sq = o * o
        hi = sq.astype(BF16)
        lo = (sq - hi.astype(F32)).astype(BF16)
        ms = (jnp.dot(hi, ones_bd, preferred_element_type=F32)
              + jnp.dot(lo, ones_bd, preferred_element_type=F32)) / RET_DK
        gate = z_ref[0, pl.ds(r0, TM), 3 * GROUP_W:4 * GROUP_W].astype(F32)
        y = o * lax.rsqrt(ms + EPS) * g_ref[...] * (gate * _sigmoid(gate))
        o_ref[0, pl.ds(r0, TM), :] = y.astype(BF16)

    c0 = n_lat * TM
    qc, kc, vc = qkv(c0)
    if ctx_out:
        finish(c0, inner(qc, kc, vc))
    else:
        o_ref[0, pl.ds(c0, TM), :] = jnp.zeros((TM, GROUP_W), BF16)

    st_ref[...] = jnp.zeros_like(st_ref)
    state_update(0, kc, vc)

    def fwd(n, carry):
        r0 = pl.multiple_of(n * TM, TM)
        q, k, v = qkv(r0)
        acc_ref[pl.ds(r0, TM), :] = inner(q, k, v) + cross(0, q)
        state_update(0, k, v)
        return carry

    lax.fori_loop(0, n_lat, fwd, 0, unroll=RET_UNROLL)

    st_ref[...] = jnp.zeros_like(st_ref)
    state_update(1, kc, vc)

    def bwd(i, carry):
        n = n_lat - 1 - i
        r0 = pl.multiple_of(n * TM, TM)
        q, k, v = qkv(r0)
        finish(r0, acc_ref[pl.ds(r0, TM), :] + cross(1, q))
        state_update(1, k, v)
        return carry

    lax.fori_loop(0, n_lat, bwd, 0, unroll=RET_UNROLL)


def _retention(zr, tabs, g, n_batch, n_lat, ctx_out):
    seq = zr.shape[1]
    dm, dq, dk, gc, bd = tabs
    full = lambda a: pl.BlockSpec(a.shape, lambda b: (0,) * a.ndim)
    return pl.pallas_call(
        functools.partial(_ret_kernel, n_lat, ctx_out),
        out_shape=jax.ShapeDtypeStruct((n_batch, seq, GROUP_W), BF16),
        grid=(n_batch,),
        in_specs=[pl.BlockSpec((1, seq, Z_RET), lambda b: (b, 0, 0)),
                  full(dm), full(dq), full(dk), full(gc), full(bd), full(g)],
        out_specs=pl.BlockSpec((1, seq, GROUP_W), lambda b: (b, 0, 0)),
        scratch_shapes=[pltpu.VMEM((n_lat * TM, GROUP_W), F32),
                        pltpu.VMEM((GROUP_W, GROUP_W), F32)],
        compiler_params=_cparams(("parallel",)),
        name="retention",
    )(zr, dm, dq, dk, gc, bd, g)


def _retention_tables(lg):
    j = jnp.arange(TM, dtype=F32)
    rel = j[:, None] - j[None, :]
    lf = lg[0][:, None, None]
    lb = lg[1][:, None, None]
    dm = (jnp.where(rel[None] >= 0, jnp.exp(jnp.maximum(rel, 0.0)[None] * lf), 0.0)
          + jnp.where(rel[None] <= 0, jnp.exp(jnp.maximum(-rel, 0.0)[None] * lb), 0.0))
    lane_f = jnp.repeat(lg[0], RET_DK)[None, :]
    lane_b = jnp.repeat(lg[1], RET_DK)[None, :]
    dq = jnp.stack([jnp.exp((j + 1)[:, None] * lane_f), jnp.exp((TM - j)[:, None] * lane_b)])
    dk = jnp.stack([jnp.exp((TM - 1 - j)[:, None] * lane_f), jnp.exp(j[:, None] * lane_b)])
    gc = jnp.stack([jnp.exp(TM * lane_f).reshape(GROUP_W, 1), jnp.exp(TM * lane_b).reshape(GROUP_W, 1)])
    hid = jnp.arange(GROUP_W) // RET_DK
    bd = (hid[:, None] == hid[None, :]).astype(F32)
    return dm, dq, dk, gc, bd


def _hy_pre_kernel(n_lat, z_ref, w_ref, b_ref, s_ref, x0_ref):
    def seg(r0, n):
        z = z_ref[0, r0:r0 + n, :].astype(F32)
        u = (_shift_rows(z, -1) * w_ref[0:1, :] + z * w_ref[1:2, :]
             + _shift_rows(z, 1) * w_ref[2:3, :] + b_ref[...])
        x0_ref[0, r0:r0 + n, :] = u[:, 0:GROUP_W].astype(BF16)
        s_ref[0, r0:r0 + n, :] = (u[:, GROUP_W:2 * GROUP_W] * u[:, 2 * GROUP_W:]).astype(BF16)

    seg(0, n_lat * TM)
    seg(n_lat * TM, z_ref.shape[1] - n_lat * TM)


def _hy_pre(zh, w, b, n_batch, n_lat):
    seq = zh.shape[1]
    return pl.pallas_call(
        functools.partial(_hy_pre_kernel, n_lat),
        out_shape=[jax.ShapeDtypeStruct((n_batch, seq, GROUP_W), BF16)] * 2,
        grid=(n_batch,),
        in_specs=[pl.BlockSpec((1, seq, Z_HY), lambda i: (i, 0, 0)),
                  pl.BlockSpec(w.shape, lambda i: (0, 0)),
                  pl.BlockSpec(b.shape, lambda i: (0, 0))],
        out_specs=[pl.BlockSpec((1, seq, GROUP_W), lambda i: (i, 0, 0))] * 2,
        compiler_params=_cparams(("parallel",)),
        name="hyena_pre",
    )(zh, w, b)


def _hy_filter_kernel(z_ref, win_ref, w1_ref, b1_ref, w2_ref, b2_ref, w3_ref, fr_ref, hf_ref, hb_ref):
    fr = fr_ref[...]
    h = jnp.sin(fr * (_fdot(z_ref[...], w1_ref[...]) + b1_ref[...]))
    h = jnp.sin(fr * (_fdot(h, w2_ref[...]) + b2_ref[...]))
    filt = _fdot(h, w3_ref[...])
    win = win_ref[...]
    hf_ref[...] = filt[:, 0:GROUP_W] * win
    hb_ref[...] = filt[:, GROUP_W:] * win


def _hy_filter(n, w1p, b1, w2, b2, w3, freq):
    t = jnp.linspace(0.0, 1.0, n, dtype=F32)[:, None]
    bands = jnp.linspace(1e-4, HY_BANDS - 1, HY_BANDS, dtype=F32)
    w = 2.0 * math.pi * jnp.arange(n, dtype=F32)[:, None] / n
    z = jnp.concatenate([t, jnp.cos(bands * w), -jnp.sin(bands * w)], axis=-1)
    z = jnp.pad(z, ((0, 0), (0, LANES - z.shape[1])))
    max_decay = math.log(HY_TARGET) / HY_FAST_PCT
    min_decay = math.log(HY_TARGET) / HY_SLOW_PCT
    deltas = jnp.abs(jnp.linspace(min_decay, max_decay, GROUP_W, dtype=F32))
    window = jnp.exp(-t * deltas)
    full = lambda a: pl.BlockSpec(a.shape, lambda i: (0, 0))
    rows = lambda wd: pl.BlockSpec((TM, wd), lambda i: (i, 0))
    return pl.pallas_call(
        _hy_filter_kernel,
        out_shape=[jax.ShapeDtypeStruct((n, GROUP_W), F32)] * 2,
        grid=(n // TM,),
        in_specs=[rows(LANES), rows(GROUP_W), full(w1p), full(b1), full(w2), full(b2), full(w3),
                  full(freq)],
        out_specs=[rows(GROUP_W)] * 2,
        compiler_params=_cparams(("arbitrary",)),
        name="hyena_filter",
    )(z, window, w1p, b1, w2, b2, w3, freq)


def _toeplitz_rows(hf, hb, nj):
    n = hf.shape[0]
    full = jnp.concatenate([hf, jnp.zeros((1, GROUP_W), F32), hb[1:][::-1]], axis=0)
    d = jnp.arange(-(nj - 1), nj)[:, None]
    m = jnp.arange(2 * TM)[None, :]
    lag = d * TM + jnp.where(m < TM, m, m - 2 * TM)
    idx = jnp.mod(lag, 2 * n)
    return jnp.transpose(full[idx], (2, 0, 1))


def _hy_conv_kernel(nj, nb, s_ref, w_ref, y_ref, acc_ref):
    def chan(c, carry):
        sc = s_ref[c].reshape(nj * nb, TM)
        acc_ref[...] = jnp.zeros_like(acc_ref)
        for di in range(2 * nj - 1):
            d = di - (nj - 1)
            w = w_ref[c, di:di + 1, :]
            rolled = pltpu.roll(jnp.broadcast_to(w, (TM, 2 * TM)), 0, 1, stride=1, stride_axis=0)
            toep = rolled[:, 0:TM].astype(BF16)
            if d >= 0:
                res = jnp.dot(sc[0:(nj - d) * nb], toep, preferred_element_type=F32)
                acc_ref[d * nb:nj * nb, :] += res
            else:
                res = jnp.dot(sc[(-d) * nb:nj * nb], toep, preferred_element_type=F32)
                acc_ref[0:(nj + d) * nb, :] += res
        y_ref[c] = acc_ref[...].reshape(nj, nb, TM).astype(BF16)
        return carry

    lax.fori_loop(0, HY_CB, chan, 0)


def _hy_conv(s_t, w_rows):
    ch, nj, nb, _ = s_t.shape
    nd = 2 * nj - 1
    return pl.pallas_call(
        functools.partial(_hy_conv_kernel, nj, nb),
        out_shape=jax.ShapeDtypeStruct(s_t.shape, BF16),
        grid=(ch // HY_CB,),
        in_specs=[pl.BlockSpec((HY_CB, nj, nb, TM), lambda i: (i, 0, 0, 0)),
                  pl.BlockSpec((HY_CB, nd, 2 * TM), lambda i: (i, 0, 0))],
        out_specs=pl.BlockSpec((HY_CB, nj, nb, TM), lambda i: (i, 0, 0, 0)),
        scratch_shapes=[pltpu.VMEM((nj * nb, TM), F32)],
        compiler_params=_cparams(("parallel",)),
        name="hyena_conv",
    )(s_t, w_rows)


def _lru_tile_scan(a, b, carry, reverse):
    row = lax.broadcasted_iota(jnp.int32, (SUBLANES, 1), 0)
    for k in (1, 2, 4):
        if reverse:
            ok = row < SUBLANES - k
            sh = SUBLANES - k
        else:
            ok = row >= k
            sh = k
        a_s = jnp.where(ok, pltpu.roll(a, sh, 0), 1.0)
        b_s = jnp.where(ok, pltpu.roll(b, sh, 0), 0.0)
        b = a * b_s + b
        a = a * a_s
    return b + a * carry


def _lru_kernel(n_lat, z_ref, cw_ref, cb_ref, wg_ref, bg_ref, sp_ref, o_ref,
                af_ref, bf_ref, ab_ref, bb_ref, h_ref):
    seq = z_ref.shape[1]
    s_lat = n_lat * TM

    def prep(r0, n):
        x = z_ref[0, r0:r0 + n, 0:GROUP_W].astype(F32)
        x = (_shift_rows(x, -2) * cw_ref[0:1, :] + _shift_rows(x, -1) * cw_ref[1:2, :]
             + x * cw_ref[2:3, :] + _shift_rows(x, 1) * cw_ref[3:4, :] + cb_ref[...])
        g = _bdot(x, wg_ref[...]) + bg_ref[...]
        for d, (a_ref, b_ref) in enumerate(((af_ref, bf_ref), (ab_ref, bb_ref))):
            r = _sigmoid(g[:, (2 * d) * GROUP_W:(2 * d + 1) * GROUP_W])
            i = _sigmoid(g[:, (2 * d + 1) * GROUP_W:(2 * d + 2) * GROUP_W])
            a = jnp.exp(-LRU_C * r * sp_ref[d:d + 1, :])
            a_ref[r0:r0 + n, :] = a
            b_ref[r0:r0 + n, :] = jnp.sqrt(1.0 - a * a) * (i * x)

    prep(0, s_lat)
    prep(s_lat, seq - s_lat)

    n_ctx_tiles = (seq - s_lat) // SUBLANES
    n_lat_tiles = s_lat // SUBLANES

    def fwd_tile(base):
        def body(i, carry):
            r0 = pl.multiple_of(base + i * SUBLANES, SUBLANES)
            h = _lru_tile_scan(af_ref[pl.ds(r0, SUBLANES), :], bf_ref[pl.ds(r0, SUBLANES), :],
                               carry, False)
            h_ref[pl.ds(r0, SUBLANES), :] = h
            return h[SUBLANES - 1:SUBLANES, :]
        return body

    carry = jnp.zeros((1, GROUP_W), F32)
    carry = lax.fori_loop(0, n_ctx_tiles, fwd_tile(s_lat), carry, unroll=SCAN_UNROLL)
    lax.fori_loop(0, n_lat_tiles, fwd_tile(0), carry, unroll=SCAN_UNROLL)

    def bwd_tile(base, n_tiles):
        def body(i, carry):
            r0 = pl.multiple_of(base + (n_tiles - 1 - i) * SUBLANES, SUBLANES)
            h = _lru_tile_scan(ab_ref[pl.ds(r0, SUBLANES), :], bb_ref[pl.ds(r0, SUBLANES), :],
                               carry, True)
            gate = z_ref[0, pl.ds(r0, SUBLANES), GROUP_W:2 * GROUP_W].astype(F32)
            y = (h + h_ref[pl.ds(r0, SUBLANES), :]) * jax.nn.gelu(gate, approximate=True)
            o_ref[0, pl.ds(r0, SUBLANES), :] = y.astype(BF16)
            return h[0:1, :]
        return body

    carry = jnp.zeros((1, GROUP_W), F32)
    carry = lax.fori_loop(0, n_ctx_tiles, bwd_tile(s_lat, n_ctx_tiles), carry, unroll=SCAN_UNROLL)
    lax.fori_loop(0, n_lat_tiles, bwd_tile(0, n_lat_tiles), carry, unroll=SCAN_UNROLL)


def _lru(zl, cw, cb, wg, bg, sp, n_batch, n_lat):
    seq = zl.shape[1]
    full = lambda a: pl.BlockSpec(a.shape, lambda i: (0,) * a.ndim)
    return pl.pallas_call(
        functools.partial(_lru_kernel, n_lat),
        out_shape=jax.ShapeDtypeStruct((n_batch, seq, GROUP_W), BF16),
        grid=(n_batch,),
        in_specs=[pl.BlockSpec((1, seq, Z_LRU), lambda i: (i, 0, 0)),
                  full(cw), full(cb), full(wg), full(bg), full(sp)],
        out_specs=pl.BlockSpec((1, seq, GROUP_W), lambda i: (i, 0, 0)),
        scratch_shapes=[pltpu.VMEM((seq, GROUP_W), F32)] * 5,
        compiler_params=_cparams(("parallel",)),
        name="rglru",
    )(zl, cw, cb, wg, bg, sp)


def _merge_kernel(x_ref, ya_ref, yb_ref, yc_ref, s_ref, x0_ref, yd_ref, hd_ref, gg_ref, wo_ref,
                  mod_ref, g2_ref, wr_ref, x1_ref, f_ref, rt_ref, cat_ref):
    s = s_ref[0].astype(F32)
    yh = x0_ref[0].astype(F32) * (yc_ref[0].astype(F32) + s * hd_ref[...])
    parts = (ya_ref[0].astype(F32), yb_ref[0].astype(F32), yh, yd_ref[0].astype(F32))
    for g, y in enumerate(parts):
        ms = jnp.mean(y * y, axis=-1, keepdims=True)
        gg = gg_ref[:, g * GROUP_W:(g + 1) * GROUP_W]
        cat_ref[:, g * GROUP_W:(g + 1) * GROUP_W] = (y * lax.rsqrt(ms + EPS) * gg).astype(BF16)
    m = jnp.dot(cat_ref[...], wo_ref[...], preferred_element_type=F32)
    gate1 = mod_ref[0, :, 2 * D_MODEL:3 * D_MODEL]
    x1 = x_ref[...] + gate1 * m
    x1_ref[...] = x1
    ms = jnp.mean(x1 * x1, axis=-1, keepdims=True)
    sh = mod_ref[0, :, 3 * D_MODEL:4 * D_MODEL]
    sc = mod_ref[0, :, 4 * D_MODEL:5 * D_MODEL]
    f = (x1 * lax.rsqrt(ms + EPS) * g2_ref[...]) * (1.0 + sc) + sh
    fb = f.astype(BF16)
    f_ref[...] = fb
    rt_ref[...] = _route(jnp.dot(fb, wr_ref[...], preferred_element_type=F32))


def _route(logits):
    lane = lax.broadcasted_iota(jnp.int32, logits.shape, 1)
    neg = jnp.float32(-jnp.inf)
    big = jnp.int32(LANES)
    gl = jnp.where(lane < MOE_GROUPS, logits, neg)
    gmax = jnp.max(gl, axis=-1, keepdims=True)
    gsum = jnp.sum(jnp.exp(gl - gmax), axis=-1, keepdims=True)
    g_val = 1.0 / gsum
    g_idx = jnp.min(jnp.where(gl == gmax, lane, big), axis=-1, keepdims=True)
    lo = MOE_GROUPS + MOE_PER_GROUP * g_idx
    el = jnp.where((lane >= lo) & (lane < lo + MOE_PER_GROUP), logits, neg)
    m1 = jnp.max(el, axis=-1, keepdims=True)
    i1 = jnp.min(jnp.where(el == m1, lane, big), axis=-1, keepdims=True)
    el2 = jnp.where(lane == i1, neg, el)
    m2 = jnp.max(el2, axis=-1, keepdims=True)
    i2 = jnp.min(jnp.where(el2 == m2, lane, big), axis=-1, keepdims=True)
    r = jnp.exp(m2 - m1)
    w1 = g_val / (1.0 + r)
    w2 = w1 * r
    e1 = (i1 - MOE_GROUPS).astype(F32)
    e2 = (i2 - MOE_GROUPS).astype(F32)
    return jnp.where(lane == 0, e1, jnp.where(lane == 1, e2, jnp.where(lane == 2, w1,
                     jnp.where(lane == 3, w2, 0.0))))


def _merge(xx, ys, hd, gg, wo, mod, g2, wr, n_batch, nt_all, nt, n_lat):
    rows_all = xx.shape[0]
    rows_out = n_batch * nt * TM
    in_rows = lambda b, t: (b * nt_all + t, 0)
    out_rows = lambda b, t: (b * nt + t, 0)
    seq_blk = pl.BlockSpec((1, TM, GROUP_W), lambda b, t: (b, t, 0))
    small = lambda a: pl.BlockSpec(a.shape, lambda b, t: (0, 0))
    return pl.pallas_call(
        _merge_kernel,
        out_shape=[jax.ShapeDtypeStruct((rows_all, D_MODEL), F32),
                   jax.ShapeDtypeStruct((rows_out, D_MODEL), BF16),
                   jax.ShapeDtypeStruct((rows_out, LANES), F32)],
        grid=(n_batch, nt),
        in_specs=[pl.BlockSpec((TM, D_MODEL), in_rows)] + [seq_blk] * 6
                 + [small(hd), small(gg), small(wo),
                    pl.BlockSpec((1, 1, 6 * D_MODEL), _mod_index(nt_all, n_lat, n_batch)),
                    small(g2), small(wr)],
        out_specs=[pl.BlockSpec((TM, D_MODEL), in_rows),
                   pl.BlockSpec((TM, D_MODEL), out_rows),
                   pl.BlockSpec((TM, LANES), out_rows)],
        scratch_shapes=[pltpu.VMEM((TM, D_MODEL), BF16)],
        input_output_aliases={0: 0},
        compiler_params=_cparams(("parallel", "arbitrary")),
        name="merge_outproj",
    )(xx, *ys, hd, gg, wo, mod, g2, wr)


def _expert_kernel(be_ref, nv_ref, x_ref, w1_ref, w3_ref, w2_ref, o_ref, w1b_ref, w3b_ref, w2b_ref):
    i = pl.program_id(0)

    @pl.when((i == 0) | (be_ref[i] != be_ref[jnp.maximum(i - 1, 0)]))
    def _():
        w1b_ref[...] = w1_ref[0, 0].astype(BF16)
        w3b_ref[...] = w3_ref[0, 0].astype(BF16)
        w2b_ref[...] = w2_ref[0, 0].astype(BF16)

    @pl.when(i < nv_ref[0])
    def _():
        x = x_ref[...]
        a = jnp.dot(x, w1b_ref[...], preferred_element_type=F32)
        b = jnp.dot(x, w3b_ref[...], preferred_element_type=F32)
        h = (a * _sigmoid(a)) * b
        o_ref[...] = jnp.dot(h.astype(BF16), w2b_ref[...], preferred_element_type=F32).astype(BF16)

    @pl.when(i >= nv_ref[0])
    def _():
        o_ref[...] = jnp.zeros_like(o_ref)


def _experts(xs, block_exp, n_valid, w1, w3, w2, layer):
    n_slots = xs.shape[0]
    n_blocks = n_slots // MOE_BM
    w_idx = lambda i, be, nv: (layer, be[i], 0, 0)
    grid_spec = pltpu.PrefetchScalarGridSpec(
        num_scalar_prefetch=2,
        grid=(n_blocks,),
        in_specs=[pl.BlockSpec((MOE_BM, D_MODEL), lambda i, be, nv: (i, 0)),
                  pl.BlockSpec((1, 1, D_MODEL, EXPERT_FF), w_idx),
                  pl.BlockSpec((1, 1, D_MODEL, EXPERT_FF), w_idx),
                  pl.BlockSpec((1, 1, EXPERT_FF, D_MODEL), w_idx)],
        out_specs=pl.BlockSpec((MOE_BM, D_MODEL), lambda i, be, nv: (i, 0)),
        scratch_shapes=[pltpu.VMEM((D_MODEL, EXPERT_FF), BF16),
                        pltpu.VMEM((D_MODEL, EXPERT_FF), BF16),
                        pltpu.VMEM((EXPERT_FF, D_MODEL), BF16)],
    )
    return pl.pallas_call(
        _expert_kernel,
        out_shape=jax.ShapeDtypeStruct((n_slots, D_MODEL), BF16),
        grid_spec=grid_spec,
        compiler_params=_cparams(("arbitrary",)),
        name="moe_experts",
    )(block_exp, n_valid, xs, w1, w3, w2)


def _combine_kernel(x_ref, ya_ref, yb_ref, rt_ref, mod_ref, o_ref):
    gate2 = mod_ref[0, :, 5 * D_MODEL:6 * D_MODEL]
    wa = rt_ref[:, 2:3]
    wb = rt_ref[:, 3:4]
    o_ref[...] = x_ref[...] + gate2 * (wa * ya_ref[...].astype(F32) + wb * yb_ref[...].astype(F32))


def _combine(xx, ya, yb, route, mod, n_batch, nt_all, nt, n_lat, in_place):
    in_rows = lambda b, t: (b * nt_all + t, 0)
    out_rows = lambda b, t: (b * nt + t, 0)
    o_rows = in_rows if in_place else out_rows
    rows = xx.shape[0] if in_place else n_batch * nt * TM
    return pl.pallas_call(
        _combine_kernel,
        out_shape=jax.ShapeDtypeStruct((rows, D_MODEL), F32),
        grid=(n_batch, nt),
        in_specs=[pl.BlockSpec((TM, D_MODEL), in_rows),
                  pl.BlockSpec((TM, D_MODEL), out_rows),
                  pl.BlockSpec((TM, D_MODEL), out_rows),
                  pl.BlockSpec((TM, LANES), out_rows),
                  pl.BlockSpec((1, 1, 6 * D_MODEL), _mod_index(nt_all, n_lat, n_batch))],
        out_specs=pl.BlockSpec((TM, D_MODEL), o_rows),
        input_output_aliases={0: 0} if in_place else {},
        compiler_params=_cparams(("parallel", "arbitrary")),
        name="moe_combine",
    )(xx, ya, yb, route, mod)


def _dispatch_plan(eid):
    t = eid.shape[0]
    n_assign = t * TOP_K
    e_flat = eid.reshape(-1)
    experts = jnp.arange(N_EXPERTS, dtype=jnp.int32)
    e_sorted, a_sorted = lax.sort((e_flat, jnp.arange(n_assign, dtype=jnp.int32)), num_keys=1,
                                  is_stable=True)
    counts = jnp.sum((e_flat[:, None] == experts[None, :]).astype(jnp.int32), axis=0)
    padded = (counts + MOE_BM - 1) // MOE_BM * MOE_BM
    pad_end = jnp.cumsum(padded)
    shift = (pad_end - padded) - (jnp.cumsum(counts) - counts)
    n_blocks = -(-n_assign // MOE_BM) + N_EXPERTS
    n_slots = n_blocks * MOE_BM
    blk0 = jnp.arange(n_blocks, dtype=jnp.int32) * MOE_BM
    block_exp = jnp.minimum(jnp.sum((pad_end[None, :] <= blk0[:, None]).astype(jnp.int32), axis=1),
                            N_EXPERTS - 1)
    n_valid = (pad_end[-1] // MOE_BM).astype(jnp.int32).reshape(1)
    block_shift = jnp.sum(jnp.where(block_exp[:, None] == experts[None, :], shift[None, :], 0), axis=1)
    src = jnp.arange(n_slots, dtype=jnp.int32) - jnp.repeat(block_shift, MOE_BM)
    slot_tok = a_sorted.at[jnp.clip(src, 0, n_assign - 1)].get(mode='promise_in_bounds') // TOP_K
    dest = (jnp.arange(n_assign, dtype=jnp.int32)
            + jnp.sum(jnp.where(e_sorted[:, None] == experts[None, :], shift[None, :], 0), axis=1))
    _, pos = lax.sort((a_sorted, dest), num_keys=1)
    return slot_tok, pos.reshape(t, TOP_K), block_exp, n_valid


def _take_rows(a, idx):
    return a.at[idx].get(mode='promise_in_bounds')


def _moe(f, route, w1, w3, w2, layer):
    eid = route[:, 0:TOP_K].astype(jnp.int32)
    slot_tok, pos, block_exp, n_valid = _dispatch_plan(eid)
    ys = _experts(_take_rows(f, slot_tok), block_exp, n_valid, w1, w3, w2, layer)
    return _take_rows(ys, pos[:, 0]), _take_rows(ys, pos[:, 1])


def _pack_w_in(w_in):
    o = 0
    cols = {}
    for name, wd in (('cq', 192), ('ckv', 128), ('kr', 32), ('ret', 1024), ('hy', 768), ('lru', 512)):
        cols[name] = w_in[:, o:o + wd]
        o += wd
    swap = jnp.arange(MLA_ROPE) ^ 8
    kr_sw = cols['kr'][:, swap]
    return jnp.concatenate([cols['cq'], cols['kr'], kr_sw, cols['ckv'], cols['ret'], cols['hy'],
                            cols['lru']], axis=1).astype(BF16)


def _pack_mla(w_uq, w_ukv, q_norm_g, kv_norm_g, qn_g, kn_g):
    hw = MLA_HEADS * LANES
    swap = jnp.arange(MLA_ROPE) ^ 8
    wq = jnp.zeros((2 * LANES, 2 * hw), F32)
    wkv = jnp.zeros((LANES, hw + MLA_HEADS * GROUP_W), F32)
    for h in range(MLA_HEADS):
        qh = w_uq[:, h * MLA_QK:(h + 1) * MLA_QK]
        wq = wq.at[:MLA_Q_RANK, h * LANES:h * LANES + MLA_QK].set(qh)
        wq = wq.at[:MLA_Q_RANK, hw + h * LANES + MLA_NOPE:hw + h * LANES + MLA_QK].set(
            qh[:, MLA_NOPE + swap])
        kvh = w_ukv[:, h * (MLA_NOPE + MLA_V):(h + 1) * (MLA_NOPE + MLA_V)]
        wkv = wkv.at[:, h * LANES:h * LANES + MLA_NOPE].set(kvh[:, :MLA_NOPE])
        v0 = hw + h * GROUP_W + h * MLA_V
        wkv = wkv.at[:, v0:v0 + MLA_V].set(kvh[:, MLA_NOPE:])
    pad = lambda v, lo, n: jnp.zeros((1, n), F32).at[0, lo:lo + v.shape[0]].set(v)
    gains = (pad(q_norm_g, 0, 2 * LANES), kv_norm_g.reshape(1, LANES),
             pad(qn_g, 0, LANES), pad(qn_g[MLA_NOPE + swap], MLA_NOPE, LANES),
             pad(kn_g[:MLA_NOPE], 0, LANES), pad(kn_g[MLA_NOPE:], MLA_NOPE, LANES),
             pad(kn_g[MLA_NOPE + swap], MLA_NOPE, LANES))
    return wq.astype(BF16), wkv.astype(BF16), gains


def _rope_tables(s_lat, seq):
    rows = s_lat // GRID_W
    row = jnp.repeat(jnp.arange(rows), GRID_W).astype(F32)
    col = jnp.tile(jnp.arange(GRID_W), rows).astype(F32)
    half = MLA_ROPE // 4
    inv_freq = ROPE_BASE ** (-jnp.arange(half, dtype=F32) / half)
    ar = row[:, None] * inv_freq
    ac = col[:, None] * inv_freq
    cos32 = jnp.concatenate([jnp.cos(ar), jnp.cos(ar), jnp.cos(ac), jnp.cos(ac)], axis=1)
    sin32 = jnp.concatenate([-jnp.sin(ar), jnp.sin(ar), -jnp.sin(ac), jnp.sin(ac)], axis=1)
    cos_t = jnp.ones((seq, LANES), F32).at[:s_lat, MLA_NOPE:MLA_QK].set(cos32)
    sin_t = jnp.zeros((seq, LANES), F32).at[:s_lat, MLA_NOPE:MLA_QK].set(sin32)
    return cos_t, sin_t


def _block_diag(w):
    nb, bw, _ = w.shape
    out = jnp.zeros((nb * bw, nb * bw), F32)
    for i in range(nb):
        out = out.at[i * bw:(i + 1) * bw, i * bw:(i + 1) * bw].set(w[i])
    return out


def kernel(x, c, ctx, c_ctx, w_mod, b_mod, norm1_g, norm2_g, w_in, mla_q_norm_g, mla_w_uq, mla_kv_norm_g, mla_w_ukv, mla_qn_g, mla_kn_g, ret_log_gamma, ret_norm_g, hy_conv_w, hy_conv_b, hy_w1, hy_b1, hy_w2, hy_b2, hy_w3, hy_freq, hy_d, lru_conv_w, lru_conv_b, lru_wa, lru_ba, lru_wx, lru_bx, lru_lambda, group_norm_g, w_out, moe_w_group, moe_w_expert, moe_w1, moe_w3, moe_w2):
    n_batch, s_lat, d = x.shape
    n_ctx = ctx.shape[1]
    depth = w_mod.shape[0]
    assert d == D_MODEL and n_ctx == TM and s_lat % TM == 0 and s_lat % GRID_W == 0
    n_lat = s_lat // TM
    nt_all = n_lat + 1
    seq = nt_all * TM
    mod_rows = -(-(n_batch + 1) // SUBLANES) * SUBLANES

    cc = jnp.zeros((mod_rows, d), F32).at[:n_batch].set(c).at[n_batch].set(c_ctx)
    mod_all = _modulation(cc, w_mod, b_mod)
    xx = jnp.concatenate([x, ctx], axis=1).reshape(n_batch * seq, d)
    cos_t, sin_t = _rope_tables(s_lat, seq)

    for l in range(depth):
        ctx_out = l < depth - 1
        nt = nt_all if ctx_out else n_lat
        mod = mod_all[l].reshape(mod_rows, 1, 6 * d)

        zm, zr, zh, zl = _inproj(xx, mod, norm1_g[l].reshape(1, d), _pack_w_in(w_in[l]),
                                 n_batch, nt_all, n_lat)
        zm = zm.reshape(n_batch, seq, Z_MLA)
        zr = zr.reshape(n_batch, seq, Z_RET)
        zh = zh.reshape(n_batch, seq, Z_HY)
        zl = zl.reshape(n_batch, seq, Z_LRU)

        wq, wkv, gains = _pack_mla(mla_w_uq[l], mla_w_ukv[l], mla_q_norm_g[l], mla_kv_norm_g[l],
                                   mla_qn_g[l], mla_kn_g[l])
        y_mla = _mla(zm, cos_t, sin_t, gains, wq, wkv, n_batch, nt, n_lat)

        y_ret = _retention(zr, _retention_tables(ret_log_gamma[l].astype(F32)),
                           ret_norm_g[l].reshape(1, GROUP_W), n_batch, n_lat, ctx_out)

        s_hy, x0_hy = _hy_pre(zh, hy_conv_w[l], hy_conv_b[l].reshape(1, Z_HY), n_batch, n_lat)
        w1p = jnp.pad(hy_w1[l], ((0, LANES - hy_w1.shape[1]), (0, 0)))
        fargs = (w1p, hy_b1[l].reshape(1, -1), hy_w2[l], hy_b2[l].reshape(1, -1), hy_w3[l],
                 hy_freq[l].reshape(1, -1))
        hf, hb = _hy_filter(s_lat, *fargs)
        s_t = jnp.transpose(s_hy[:, :s_lat].reshape(n_batch, n_lat, TM, GROUP_W), (3, 1, 0, 2))
        y_t = _hy_conv(s_t, _toeplitz_rows(hf, hb, n_lat))
        y_hy = jnp.transpose(y_t, (2, 1, 3, 0)).reshape(n_batch, s_lat, GROUP_W)
        if ctx_out:
            hf_c, hb_c = _hy_filter(n_ctx, *fargs)
            sc_t = jnp.transpose(s_hy[:, s_lat:].reshape(n_batch, 1, TM, GROUP_W), (3, 1, 0, 2))
            yc_t = _hy_conv(sc_t, _toeplitz_rows(hf_c, hb_c, 1))
            y_hy_c = jnp.transpose(yc_t, (2, 1, 3, 0)).reshape(n_batch, n_ctx, GROUP_W)
        else:
            y_hy_c = jnp.zeros((n_batch, n_ctx, GROUP_W), BF16)
        y_hy = jnp.concatenate([y_hy, y_hy_c], axis=1)

        wg = jnp.concatenate([_block_diag(lru_wa[l, 0]), _block_diag(lru_wx[l, 0]),
                              _block_diag(lru_wa[l, 1]), _block_diag(lru_wx[l, 1])], axis=1).astype(BF16)
        bg = jnp.concatenate([lru_ba[l, 0], lru_bx[l, 0], lru_ba[l, 1], lru_bx[l, 1]]).reshape(1, -1)
        sp = jax.nn.softplus(-lru_lambda[l].astype(F32))
        y_lru = _lru(zl, lru_conv_w[l], lru_conv_b[l].reshape(1, -1), wg, bg, sp, n_batch, n_lat)

        wr = jnp.zeros((d, LANES), F32).at[:, :MOE_GROUPS].set(moe_w_group[l])
        wr = wr.at[:, MOE_GROUPS:MOE_GROUPS + N_EXPERTS].set(moe_w_expert[l]).astype(BF16)
        xx, f, route = _merge(xx, (y_mla, y_ret, y_hy, s_hy, x0_hy, y_lru), hy_d[l].reshape(1, -1),
                               group_norm_g[l].reshape(1, -1), w_out[l].astype(BF16), mod,
                               norm2_g[l].reshape(1, d), wr, n_batch, nt_all, nt, n_lat)

        ya, yb = _moe(f, route, moe_w1, moe_w3, moe_w2, l)
        xx = _combine(xx, ya, yb, route, mod, n_batch, nt_all, nt, n_lat, in_place=ctx_out)

    return xx.reshape(n_batch, s_lat, d)
```

```python
import functools
import math

import jax
import jax.numpy as jnp
from jax import lax
from jax.experimental import pallas as pl
from jax.experimental.pallas import tpu as pltpu

F32 = jnp.float32
BF16 = jnp.bfloat16

D_MODEL = 1024
EPS = 1e-6
GRID_W = 64
N_GROUPS = 4
GROUP_W = 256

MLA_HEADS = 4
MLA_NOPE = 64
MLA_ROPE = 32
MLA_QK = 96
MLA_V = 64
MLA_Q_RANK = 192
MLA_KV_RANK = 128
ROPE_BASE = 10000.0

RET_HEADS = 4
RET_DK = 64

HY_BANDS = 16
HY_FAST_PCT = 0.3
HY_SLOW_PCT = 1.5
HY_TARGET = 1e-2

LRU_C = 8.0

MOE_GROUPS = 4
MOE_PER_GROUP = 8
N_EXPERTS = 32
TOP_K = 2
EXPERT_FF = 512

LANES = 128
SUBLANES = 8
MXU_DIM = 256

TM = 256
MLA_TQ = 512
MOE_BM = 512
MOE_SPLIT = 2
HY_CB = 8
SCAN_UNROLL = 8
RET_UNROLL = 4
VMEM_LIMIT = 56 * 1024 * 1024

Z_MLA, Z_RET, Z_HY, Z_LRU = 384, 1024, 768, 512
Z_COLS = Z_MLA + Z_RET + Z_HY + Z_LRU


def _cparams(sem):
    return pltpu.CompilerParams(dimension_semantics=sem, vmem_limit_bytes=VMEM_LIMIT)


def _bdot(a, b):
    return jnp.dot(a.astype(BF16), b.astype(BF16), preferred_element_type=F32)


def _bdot_t(a, b):
    return lax.dot_general(a.astype(BF16), b.astype(BF16), (((1,), (1,)), ((), ())),
                           preferred_element_type=F32)


def _fdot(a, b):
    return jnp.dot(a, b, preferred_element_type=F32, precision=lax.Precision.HIGHEST)


def _sigmoid(x):
    return 0.5 * jnp.tanh(0.5 * x) + 0.5


def _shift_rows(x, d):
    n = x.shape[0]
    r = pltpu.roll(x, (n - d) % n, 0)
    row = lax.broadcasted_iota(jnp.int32, (n, 1), 0)
    ok = (row + d >= 0) & (row + d < n)
    return jnp.where(ok, r, 0.0)


def _mod_kernel(c_ref, w_ref, b_ref, o_ref):
    c = c_ref[...]
    a = c * jax.nn.sigmoid(c)
    o_ref[0] = _fdot(a, w_ref[0]) + b_ref[0]


def _modulation(cc, w_mod, b_mod):
    nl, d, n6 = w_mod.shape
    rows = cc.shape[0]
    tn = 1536
    return pl.pallas_call(
        _mod_kernel,
        out_shape=jax.ShapeDtypeStruct((nl, rows, n6), F32),
        grid=(nl, n6 // tn),
        in_specs=[pl.BlockSpec((rows, d), lambda l, j: (0, 0)),
                  pl.BlockSpec((1, d, tn), lambda l, j: (l, 0, j)),
                  pl.BlockSpec((1, 1, tn), lambda l, j: (l, 0, j))],
        out_specs=pl.BlockSpec((1, rows, tn), lambda l, j: (l, 0, j)),
        compiler_params=_cparams(("arbitrary", "arbitrary")),
        name="modulation",
    )(cc, w_mod, b_mod.reshape(nl, 1, n6))


def _stream_specs(src, nt_all, n_lat):
    if len(src) == 1:
        return [pl.BlockSpec((TM, D_MODEL), lambda b, t: (b * nt_all + t, 0))]
    return [pl.BlockSpec((1, TM, D_MODEL), lambda b, t: (b, jnp.minimum(t, n_lat - 1), 0)),
            pl.BlockSpec((1, TM, D_MODEL), lambda b, t: (b, 0, 0))]


def _stream_tile(refs, n_lat):
    if len(refs) == 1:
        return refs[0][...]
    return jnp.where(pl.program_id(1) < n_lat, refs[0][0], refs[1][0])


def _inproj_kernel(n_src, n_lat, *refs):
    x = _stream_tile(refs[:n_src], n_lat)
    mod_ref, g_ref, w_ref, zm_ref, zr_ref, zh_ref, zl_ref = refs[n_src:]
    ms = jnp.mean(x * x, axis=-1, keepdims=True)
    y = x * lax.rsqrt(ms + EPS) * g_ref[...]
    sh = mod_ref[0, :, 0:D_MODEL]
    sc = mod_ref[0, :, D_MODEL:2 * D_MODEL]
    a = y * (1.0 + sc) + sh
    z = _bdot(a, w_ref[...])
    o = 0
    for ref, w in ((zm_ref, Z_MLA), (zr_ref, Z_RET), (zh_ref, Z_HY), (zl_ref, Z_LRU)):
        ref[...] = z[:, o:o + w].astype(BF16)
        o += w


def _mod_index(nt_all, n_lat, n_batch):
    def idx(b, t):
        return (jnp.where(t < n_lat, b, n_batch), 0, 0)
    return idx


def _inproj(src, mod, g1, w_in_p, n_batch, nt, n_lat):
    rows = n_batch * nt * TM
    row_map = lambda b, t: (b * nt + t, 0)
    outs = [jax.ShapeDtypeStruct((rows, w), BF16) for w in (Z_MLA, Z_RET, Z_HY, Z_LRU)]
    return pl.pallas_call(
        functools.partial(_inproj_kernel, len(src), n_lat),
        out_shape=outs,
        grid=(n_batch, nt),
        in_specs=_stream_specs(src, nt, n_lat)
                 + [pl.BlockSpec((1, 1, 6 * D_MODEL), _mod_index(nt, n_lat, n_batch)),
                    pl.BlockSpec((1, D_MODEL), lambda b, t: (0, 0)),
                    pl.BlockSpec((D_MODEL, Z_COLS), lambda b, t: (0, 0))],
        out_specs=[pl.BlockSpec((TM, w), row_map) for w in (Z_MLA, Z_RET, Z_HY, Z_LRU)],
        compiler_params=_cparams(("parallel", "arbitrary")),
        name="inproj",
    )(*src, mod, g1, w_in_p)


def _mla_kernel(tq, q_row0, key_row0, n_key_tiles, z_ref, cos_ref, sin_ref, gq_ref, gkv_ref, gqm_ref,
                gqs_ref, gkm_ref, gkr_ref, gks_ref, wq_ref, wkv_ref, o_ref, k_scr, v_scr):
    t = pl.program_id(1)
    hw = MLA_HEADS * LANES
    lane = lax.broadcasted_iota(jnp.int32, (1, LANES), 1)
    is_rope = (lane >= MLA_NOPE) & (lane < MLA_QK)
    vlane = lax.broadcasted_iota(jnp.int32, (1, GROUP_W), 1)

    def kv_tile(i, carry):
        r0 = pl.multiple_of(i * TM, TM)
        rs0 = pl.multiple_of(key_row0 + i * TM, TM)
        z = z_ref[0, pl.ds(rs0, TM), LANES:3 * LANES].astype(F32)
        col1 = z[:, 0:LANES]
        zb = z[:, LANES:2 * LANES]
        cos = cos_ref[pl.ds(rs0, TM), :]
        sin = sin_ref[pl.ds(rs0, TM), :]
        ms_kv = jnp.mean(zb * zb, axis=-1, keepdims=True)
        kv = _bdot(zb * lax.rsqrt(ms_kv + EPS) * gkv_ref[...], wkv_ref[...])
        kr = jnp.where(is_rope, col1, 0.0)
        kr_sw = jnp.where(is_rope, pltpu.roll(col1, LANES - MLA_ROPE, 1), 0.0)
        k_rot = kr * gkr_ref[...] * cos + kr_sw * gks_ref[...] * sin
        ss_kr = jnp.sum(kr * kr, axis=-1, keepdims=True)
        for h in range(MLA_HEADS):
            kn = kv[:, h * LANES:(h + 1) * LANES]
            rk = lax.rsqrt((jnp.sum(kn * kn, axis=-1, keepdims=True) + ss_kr) / MLA_QK + EPS)
            k_scr[pl.ds(r0, TM), h * LANES:(h + 1) * LANES] = (
                rk * (kn * gkm_ref[...] + k_rot)).astype(BF16)
            vh = kv[:, hw + h * GROUP_W:hw + (h + 1) * GROUP_W]
            ones_lane = ((h + 1) % MLA_HEADS) * MLA_V
            v_scr[pl.ds(r0, TM), h * GROUP_W:(h + 1) * GROUP_W] = jnp.where(
                vlane == ones_lane, 1.0, vh).astype(BF16)
        return carry

    @pl.when(t == 0)
    def _():
        lax.fori_loop(0, n_key_tiles, kv_tile, 0)

    q0 = pl.multiple_of(q_row0 + t * tq, TM)
    za = z_ref[0, pl.ds(q0, tq), 0:2 * LANES].astype(F32)
    cos = cos_ref[pl.ds(q0, tq), :]
    sin = sin_ref[pl.ds(q0, tq), :]
    lane2 = lax.broadcasted_iota(jnp.int32, (1, 2 * LANES), 1)
    ms_q = jnp.sum(jnp.where(lane2 < MLA_Q_RANK, za * za, 0.0), axis=-1, keepdims=True) / MLA_Q_RANK
    qall = _bdot(za * lax.rsqrt(ms_q + EPS) * gq_ref[...], wq_ref[...])
    q_heads = []
    for h in range(MLA_HEADS):
        qh = qall[:, h * LANES:(h + 1) * LANES]
        qs = qall[:, hw + h * LANES:hw + (h + 1) * LANES]
        rs = lax.rsqrt(jnp.sum(qh * qh, axis=-1, keepdims=True) / MLA_QK + EPS) * (MLA_QK ** -0.5)
        q_heads.append((rs * (qh * gqm_ref[...] * cos + qs * gqs_ref[...] * sin)).astype(BF16))

    acc = jnp.zeros((tq, GROUP_W), F32)
    for h in range(MLA_HEADS):
        s = _bdot_t(q_heads[h], k_scr[:, h * LANES:(h + 1) * LANES])
        m = jnp.max(s, axis=-1, keepdims=True)
        p = jnp.exp((s - m).astype(BF16))
        pv = jnp.dot(p, v_scr[:, h * GROUP_W:(h + 1) * GROUP_W], preferred_element_type=F32)
        ones_lane = ((h + 1) % MLA_HEADS) * MLA_V
        l = jnp.sum(jnp.where(vlane == ones_lane, pv, 0.0), axis=-1, keepdims=True)
        in_head = (vlane >= h * MLA_V) & (vlane < (h + 1) * MLA_V)
        acc = acc + jnp.where(in_head, pv * (1.0 / l), 0.0)
    o_ref[0] = acc.astype(BF16)


def _mla(zm, cos_t, sin_t, gains, wq, wkv, n_batch, tq, q_row0, n_q_tiles, key_row0, n_key_tiles):
    seq = zm.shape[1]
    hw = MLA_HEADS * LANES
    vw = MLA_HEADS * GROUP_W
    n_keys = n_key_tiles * TM
    small = lambda w: pl.BlockSpec((1, w), lambda b, t: (0, 0))
    return pl.pallas_call(
        functools.partial(_mla_kernel, tq, q_row0, key_row0, n_key_tiles),
        out_shape=jax.ShapeDtypeStruct((n_batch, n_q_tiles * tq, GROUP_W), BF16),
        grid=(n_batch, n_q_tiles),
        in_specs=[pl.BlockSpec((1, seq, Z_MLA), lambda b, t: (b, 0, 0)),
                  pl.BlockSpec((seq, LANES), lambda b, t: (0, 0)),
                  pl.BlockSpec((seq, LANES), lambda b, t: (0, 0)),
                  small(2 * LANES), small(LANES), small(LANES), small(LANES), small(LANES),
                  small(LANES), small(LANES),
                  pl.BlockSpec((2 * LANES, 2 * hw), lambda b, t: (0, 0)),
                  pl.BlockSpec((LANES, hw + vw), lambda b, t: (0, 0))],
        out_specs=pl.BlockSpec((1, tq, GROUP_W), lambda b, t: (b, t, 0)),
        scratch_shapes=[pltpu.VMEM((n_keys, hw), BF16), pltpu.VMEM((n_keys, vw), BF16)],
        compiler_params=_cparams(("parallel", "arbitrary")),
        name="mla_attention",
    )(zm, cos_t, sin_t, *gains, wq, wkv)


def _ret_kernel(n_lat, ctx_out, z_ref, dm_ref, dq_ref, dk_ref, gc_ref, bd_ref, g_ref,
                o_ref, acc_ref, st_ref):
    lane = lax.broadcasted_iota(jnp.int32, (1, GROUP_W), 1)
    hmask = [(lane >= h * RET_DK) & (lane < (h + 1) * RET_DK) for h in range(RET_HEADS)]
    ones_bd = bd_ref[...].astype(BF16)

    def qkv(r0):
        q = z_ref[0, pl.ds(r0, TM), 0:GROUP_W]
        k = z_ref[0, pl.ds(r0, TM), GROUP_W:2 * GROUP_W].astype(F32) * (RET_DK ** -0.5)
        v = z_ref[0, pl.ds(r0, TM), 2 * GROUP_W:3 * GROUP_W]
        return q, k, v

    def inner(q, k, v):
        kb = k.astype(BF16)
        acc = jnp.zeros((TM, GROUP_W), F32)
        for h in range(RET_HEADS):
            qh = jnp.where(hmask[h], q, jnp.zeros_like(q))
            vh = jnp.where(hmask[h], v, jnp.zeros_like(v))
            s = _bdot_t(qh, kb) * dm_ref[h]
            acc = acc + _bdot(s, vh)
        return acc

    def state_update(d, k, v):
        kd = (k * dk_ref[d]).T
        st_ref[...] = st_ref[...] * gc_ref[d] + _bdot(kd, v) * bd_ref[...]

    def cross(d, q):
        return _bdot(q.astype(F32) * dq_ref[d], st_ref[...])

    def finish(r0, o):
        sq = o * o
        hi = sq.astype(BF16)
        lo = (sq - hi.astype(F32)).astype(BF16)
        ms = (jnp.dot(hi, ones_bd, preferred_element_type=F32)
              + jnp.dot(lo, ones_bd, preferred_element_type=F32)) / RET_DK
        gate = z_ref[0, pl.ds(r0, TM), 3 * GROUP_W:4 * GROUP_W].astype(F32)
        y = o * lax.rsqrt(ms + EPS) * g_ref[...] * (gate * _sigmoid(gate))
        o_ref[0, pl.ds(r0, TM), :] = y.astype(BF16)

    c0 = n_lat * TM
    qc, kc, vc = qkv(c0)
    if ctx_out:
        finish(c0, inner(qc, kc, vc))
    else:
        o_ref[0, pl.ds(c0, TM), :] = jnp.zeros((TM, GROUP_W), BF16)

    st_ref[...] = jnp.zeros_like(st_ref)
    state_update(0, kc, vc)

    def fwd(n, carry):
        r0 = pl.multiple_of(n * TM, TM)
        q, k, v = qkv(r0)
        acc_ref[pl.ds(r0, TM), :] = inner(q, k, v) + cross(0, q)
        state_update(0, k, v)
        return carry

    lax.fori_loop(0, n_lat, fwd, 0, unroll=RET_UNROLL)

    st_ref[...] = jnp.zeros_like(st_ref)
    state_update(1, kc, vc)

    def bwd(i, carry):
        n = n_lat - 1 - i
        r0 = pl.multiple_of(n * TM, TM)
        q, k, v = qkv(r0)
        finish(r0, acc_ref[pl.ds(r0, TM), :] + cross(1, q))
        state_update(1, k, v)
        return carry

    lax.fori_loop(0, n_lat, bwd, 0, unroll=RET_UNROLL)


def _retention(zr, tabs, g, n_batch, n_lat, ctx_out):
    seq = zr.shape[1]
    dm, dq, dk, gc, bd = tabs
    full = lambda a: pl.BlockSpec(a.shape, lambda b: (0,) * a.ndim)
    return pl.pallas_call(
        functools.partial(_ret_kernel, n_lat, ctx_out),
        out_shape=jax.ShapeDtypeStruct((n_batch, seq, GROUP_W), BF16),
        grid=(n_batch,),
        in_specs=[pl.BlockSpec((1, seq, Z_RET), lambda b: (b, 0, 0)),
                  full(dm), full(dq), full(dk), full(gc), full(bd), full(g)],
        out_specs=pl.BlockSpec((1, seq, GROUP_W), lambda b: (b, 0, 0)),
        scratch_shapes=[pltpu.VMEM((n_lat * TM, GROUP_W), F32),
                        pltpu.VMEM((GROUP_W, GROUP_W), F32)],
        compiler_params=_cparams(("parallel",)),
        name="retention",
    )(zr, dm, dq, dk, gc, bd, g)


def _retention_tables(lg):
    j = jnp.arange(TM, dtype=F32)
    rel = j[:, None] - j[None, :]
    lf = lg[0][:, None, None]
    lb = lg[1][:, None, None]
    dm = (jnp.where(rel[None] >= 0, jnp.exp(jnp.maximum(rel, 0.0)[None] * lf), 0.0)
          + jnp.where(rel[None] <= 0, jnp.exp(jnp.maximum(-rel, 0.0)[None] * lb), 0.0))
    lane_f = jnp.repeat(lg[0], RET_DK)[None, :]
    lane_b = jnp.repeat(lg[1], RET_DK)[None, :]
    dq = jnp.stack([jnp.exp((j + 1)[:, None] * lane_f), jnp.exp((TM - j)[:, None] * lane_b)])
    dk = jnp.stack([jnp.exp((TM - 1 - j)[:, None] * lane_f), jnp.exp(j[:, None] * lane_b)])
    gc = jnp.stack([jnp.exp(TM * lane_f).reshape(GROUP_W, 1), jnp.exp(TM * lane_b).reshape(GROUP_W, 1)])
    hid = jnp.arange(GROUP_W) // RET_DK
    bd = (hid[:, None] == hid[None, :]).astype(F32)
    return dm, dq, dk, gc, bd


def _hy_pre_kernel(n_lat, z_ref, w_ref, b_ref, s_ref, x0_ref):
    def seg(r0, n):
        z = z_ref[0, r0:r0 + n, :].astype(F32)
        u = (_shift_rows(z, -1) * w_ref[0:1, :] + z * w_ref[1:2, :]
             + _shift_rows(z, 1) * w_ref[2:3, :] + b_ref[...])
        x0_ref[0, r0:r0 + n, :] = u[:, 0:GROUP_W].astype(BF16)
        s_ref[0, r0:r0 + n, :] = (u[:, GROUP_W:2 * GROUP_W] * u[:, 2 * GROUP_W:]).astype(BF16)

    seg(0, n_lat * TM)
    seg(n_lat * TM, z_ref.shape[1] - n_lat * TM)


def _hy_pre(zh, w, b, n_batch, n_lat):
    seq = zh.shape[1]
    return pl.pallas_call(
        functools.partial(_hy_pre_kernel, n_lat),
        out_shape=[jax.ShapeDtypeStruct((n_batch, seq, GROUP_W), BF16)] * 2,
        grid=(n_batch,),
        in_specs=[pl.BlockSpec((1, seq, Z_HY), lambda i: (i, 0, 0)),
                  pl.BlockSpec(w.shape, lambda i: (0, 0)),
                  pl.BlockSpec(b.shape, lambda i: (0, 0))],
        out_specs=[pl.BlockSpec((1, seq, GROUP_W), lambda i: (i, 0, 0))] * 2,
        compiler_params=_cparams(("parallel",)),
        name="hyena_pre",
    )(zh, w, b)


def _hy_filter_kernel(z_ref, win_ref, w1_ref, b1_ref, w2_ref, b2_ref, w3_ref, fr_ref, hf_ref, hb_ref):
    fr = fr_ref[...]
    h = jnp.sin(fr * (_fdot(z_ref[...], w1_ref[...]) + b1_ref[...]))
    h = jnp.sin(fr * (_fdot(h, w2_ref[...]) + b2_ref[...]))
    filt = _fdot(h, w3_ref[...])
    win = win_ref[...]
    hf_ref[...] = filt[:, 0:GROUP_W] * win
    hb_ref[...] = filt[:, GROUP_W:] * win


def _hy_filter(n, w1p, b1, w2, b2, w3, freq):
    t = jnp.linspace(0.0, 1.0, n, dtype=F32)[:, None]
    bands = jnp.linspace(1e-4, HY_BANDS - 1, HY_BANDS, dtype=F32)
    w = 2.0 * math.pi * jnp.arange(n, dtype=F32)[:, None] / n
    z = jnp.concatenate([t, jnp.cos(bands * w), -jnp.sin(bands * w)], axis=-1)
    z = jnp.pad(z, ((0, 0), (0, LANES - z.shape[1])))
    max_decay = math.log(HY_TARGET) / HY_FAST_PCT
    min_decay = math.log(HY_TARGET) / HY_SLOW_PCT
    deltas = jnp.abs(jnp.linspace(min_decay, max_decay, GROUP_W, dtype=F32))
    window = jnp.exp(-t * deltas)
    full = lambda a: pl.BlockSpec(a.shape, lambda i: (0, 0))
    rows = lambda wd: pl.BlockSpec((TM, wd), lambda i: (i, 0))
    return pl.pallas_call(
        _hy_filter_kernel,
        out_shape=[jax.ShapeDtypeStruct((n, GROUP_W), F32)] * 2,
        grid=(n // TM,),
        in_specs=[rows(LANES), rows(GROUP_W), full(w1p), full(b1), full(w2), full(b2), full(w3),
                  full(freq)],
        out_specs=[rows(GROUP_W)] * 2,
        compiler_params=_cparams(("arbitrary",)),
        name="hyena_filter",
    )(z, window, w1p, b1, w2, b2, w3, freq)


def _toeplitz_rows(hf, hb, nj):
    n = hf.shape[0]
    full = jnp.concatenate([hf, jnp.zeros((1, GROUP_W), F32), hb[1:][::-1]], axis=0)
    d = jnp.arange(-(nj - 1), nj)[:, None]
    m = jnp.arange(2 * TM)[None, :]
    lag = d * TM + jnp.where(m < TM, m, m - 2 * TM)
    idx = jnp.mod(lag, 2 * n)
    return jnp.transpose(full[idx], (2, 0, 1))


def _hy_conv_kernel(nj, nb, s_ref, w_ref, y_ref, acc_ref):
    def chan(c, carry):
        sc = s_ref[c].reshape(nj * nb, TM)
        acc_ref[...] = jnp.zeros_like(acc_ref)
        for di in range(2 * nj - 1):
            d = di - (nj - 1)
            w = w_ref[c, di:di + 1, :]
            rolled = pltpu.roll(jnp.broadcast_to(w, (TM, 2 * TM)), 0, 1, stride=1, stride_axis=0)
            toep = rolled[:, 0:TM].astype(BF16)
            if d >= 0:
                res = jnp.dot(sc[0:(nj - d) * nb], toep, preferred_element_type=F32)
                acc_ref[d * nb:nj * nb, :] += res
            else:
                res = jnp.dot(sc[(-d) * nb:nj * nb], toep, preferred_element_type=F32)
                acc_ref[0:(nj + d) * nb, :] += res
        y_ref[c] = acc_ref[...].reshape(nj, nb, TM).astype(BF16)
        return carry

    lax.fori_loop(0, HY_CB, chan, 0)


def _hy_conv(s_t, w_rows):
    ch, nj, nb, _ = s_t.shape
    nd = 2 * nj - 1
    return pl.pallas_call(
        functools.partial(_hy_conv_kernel, nj, nb),
        out_shape=jax.ShapeDtypeStruct(s_t.shape, BF16),
        grid=(ch // HY_CB,),
        in_specs=[pl.BlockSpec((HY_CB, nj, nb, TM), lambda i: (i, 0, 0, 0)),
                  pl.BlockSpec((HY_CB, nd, 2 * TM), lambda i: (i, 0, 0))],
        out_specs=pl.BlockSpec((HY_CB, nj, nb, TM), lambda i: (i, 0, 0, 0)),
        scratch_shapes=[pltpu.VMEM((nj * nb, TM), F32)],
        compiler_params=_cparams(("parallel",)),
        name="hyena_conv",
    )(s_t, w_rows)


def _lru_tile_scan(a, b, carry, reverse):
    row = lax.broadcasted_iota(jnp.int32, (SUBLANES, 1), 0)
    for k in (1, 2, 4):
        if reverse:
            ok = row < SUBLANES - k
            sh = SUBLANES - k
        else:
            ok = row >= k
            sh = k
        a_s = jnp.where(ok, pltpu.roll(a, sh, 0), 1.0)
        b_s = jnp.where(ok, pltpu.roll(b, sh, 0), 0.0)
        b = a * b_s + b
        a = a * a_s
    return b + a * carry


def _lru_kernel(n_lat, z_ref, cw_ref, cb_ref, wg_ref, bg_ref, sp_ref, o_ref,
                af_ref, bf_ref, ab_ref, bb_ref, h_ref):
    seq = z_ref.shape[1]
    s_lat = n_lat * TM

    def prep(r0, n):
        x = z_ref[0, r0:r0 + n, 0:GROUP_W].astype(F32)
        x = (_shift_rows(x, -2) * cw_ref[0:1, :] + _shift_rows(x, -1) * cw_ref[1:2, :]
             + x * cw_ref[2:3, :] + _shift_rows(x, 1) * cw_ref[3:4, :] + cb_ref[...])
        g = _bdot(x, wg_ref[...]) + bg_ref[...]
        for d, (a_ref, b_ref) in enumerate(((af_ref, bf_ref), (ab_ref, bb_ref))):
            r = _sigmoid(g[:, (2 * d) * GROUP_W:(2 * d + 1) * GROUP_W])
            i = _sigmoid(g[:, (2 * d + 1) * GROUP_W:(2 * d + 2) * GROUP_W])
            a = jnp.exp(-LRU_C * r * sp_ref[d:d + 1, :])
            a_ref[r0:r0 + n, :] = a
            b_ref[r0:r0 + n, :] = jnp.sqrt(1.0 - a * a) * (i * x)

    prep(0, s_lat)
    prep(s_lat, seq - s_lat)

    n_ctx_tiles = (seq - s_lat) // SUBLANES
    n_lat_tiles = s_lat // SUBLANES

    def fwd_tile(base):
        def body(i, carry):
            r0 = pl.multiple_of(base + i * SUBLANES, SUBLANES)
            h = _lru_tile_scan(af_ref[pl.ds(r0, SUBLANES), :], bf_ref[pl.ds(r0, SUBLANES), :],
                               carry, False)
            h_ref[pl.ds(r0, SUBLANES), :] = h
            return h[SUBLANES - 1:SUBLANES, :]
        return body

    carry = jnp.zeros((1, GROUP_W), F32)
    carry = lax.fori_loop(0, n_ctx_tiles, fwd_tile(s_lat), carry, unroll=SCAN_UNROLL)
    lax.fori_loop(0, n_lat_tiles, fwd_tile(0), carry, unroll=SCAN_UNROLL)

    def bwd_tile(base, n_tiles):
        def body(i, carry):
            r0 = pl.multiple_of(base + (n_tiles - 1 - i) * SUBLANES, SUBLANES)
            h = _lru_tile_scan(ab_ref[pl.ds(r0, SUBLANES), :], bb_ref[pl.ds(r0, SUBLANES), :],
                               carry, True)
            gate = z_ref[0, pl.ds(r0, SUBLANES), GROUP_W:2 * GROUP_W].astype(F32)
            y = (h + h_ref[pl.ds(r0, SUBLANES), :]) * jax.nn.gelu(gate, approximate=True)
            o_ref[0, pl.ds(r0, SUBLANES), :] = y.astype(BF16)
            return h[0:1, :]
        return body

    carry = jnp.zeros((1, GROUP_W), F32)
    carry = lax.fori_loop(0, n_ctx_tiles, bwd_tile(s_lat, n_ctx_tiles), carry, unroll=SCAN_UNROLL)
    lax.fori_loop(0, n_lat_tiles, bwd_tile(0, n_lat_tiles), carry, unroll=SCAN_UNROLL)


def _lru(zl, cw, cb, wg, bg, sp, n_batch, n_lat):
    seq = zl.shape[1]
    full = lambda a: pl.BlockSpec(a.shape, lambda i: (0,) * a.ndim)
    return pl.pallas_call(
        functools.partial(_lru_kernel, n_lat),
        out_shape=jax.ShapeDtypeStruct((n_batch, seq, GROUP_W), BF16),
        grid=(n_batch,),
        in_specs=[pl.BlockSpec((1, seq, Z_LRU), lambda i: (i, 0, 0)),
                  full(cw), full(cb), full(wg), full(bg), full(sp)],
        out_specs=pl.BlockSpec((1, seq, GROUP_W), lambda i: (i, 0, 0)),
        scratch_shapes=[pltpu.VMEM((seq, GROUP_W), F32)] * 5,
        compiler_params=_cparams(("parallel",)),
        name="rglru",
    )(zl, cw, cb, wg, bg, sp)


def _merge_kernel(n_src, n_lat, *refs):
    x_in = _stream_tile(refs[:n_src], n_lat)
    (ya_ref, yb_ref, yc_ref, s_ref, x0_ref, yd_ref, hd_ref, gg_ref, wo_ref,
     mod_ref, g2_ref, wr_ref, x1_ref, f_ref, rt_ref, cat_ref) = refs[n_src:]
    s = s_ref[0].astype(F32)
    yh = x0_ref[0].astype(F32) * (yc_ref[0].astype(F32) + s * hd_ref[...])
    parts = (ya_ref[0].astype(F32), yb_ref[0].astype(F32), yh, yd_ref[0].astype(F32))
    for g, y in enumerate(parts):
        ms = jnp.mean(y * y, axis=-1, keepdims=True)
        gg = gg_ref[:, g * GROUP_W:(g + 1) * GROUP_W]
        cat_ref[:, g * GROUP_W:(g + 1) * GROUP_W] = (y * lax.rsqrt(ms + EPS) * gg).astype(BF16)
    m = jnp.dot(cat_ref[...], wo_ref[...], preferred_element_type=F32)
    gate1 = mod_ref[0, :, 2 * D_MODEL:3 * D_MODEL]
    x1 = x_in + gate1 * m
    x1_ref[...] = x1
    ms = jnp.mean(x1 * x1, axis=-1, keepdims=True)
    sh = mod_ref[0, :, 3 * D_MODEL:4 * D_MODEL]
    sc = mod_ref[0, :, 4 * D_MODEL:5 * D_MODEL]
    f = (x1 * lax.rsqrt(ms + EPS) * g2_ref[...]) * (1.0 + sc) + sh
    fb = f.astype(BF16)
    f_ref[...] = fb
    rt_ref[...] = _route(jnp.dot(fb, wr_ref[...], preferred_element_type=F32))


def _route(logits):
    lane = lax.broadcasted_iota(jnp.int32, logits.shape, 1)
    neg = jnp.float32(-jnp.inf)
    big = jnp.int32(LANES)
    gl = jnp.where(lane < MOE_GROUPS, logits, neg)
    gmax = jnp.max(gl, axis=-1, keepdims=True)
    gsum = jnp.sum(jnp.exp(gl - gmax), axis=-1, keepdims=True)
    g_val = 1.0 / gsum
    g_idx = jnp.min(jnp.where(gl == gmax, lane, big), axis=-1, keepdims=True)
    lo = MOE_GROUPS + MOE_PER_GROUP * g_idx
    el = jnp.where((lane >= lo) & (lane < lo + MOE_PER_GROUP), logits, neg)
    m1 = jnp.max(el, axis=-1, keepdims=True)
    i1 = jnp.min(jnp.where(el == m1, lane, big), axis=-1, keepdims=True)
    el2 = jnp.where(lane == i1, neg, el)
    m2 = jnp.max(el2, axis=-1, keepdims=True)
    i2 = jnp.min(jnp.where(el2 == m2, lane, big), axis=-1, keepdims=True)
    r = jnp.exp(m2 - m1)
    w1 = g_val / (1.0 + r)
    w2 = w1 * r
    e1 = (i1 - MOE_GROUPS).astype(F32)
    e2 = (i2 - MOE_GROUPS).astype(F32)
    return jnp.where(lane == 0, e1, jnp.where(lane == 1, e2, jnp.where(lane == 2, w1,
                     jnp.where(lane == 3, w2, 0.0))))


def _merge(src, ys, hd, gg, wo, mod, g2, wr, n_batch, nt_all, nt, n_lat):
    rows_all = n_batch * nt_all * TM
    rows_out = n_batch * nt * TM
    in_rows = lambda b, t: (b * nt_all + t, 0)
    out_rows = lambda b, t: (b * nt + t, 0)
    seq_blk = pl.BlockSpec((1, TM, GROUP_W), lambda b, t: (b, t, 0))
    small = lambda a: pl.BlockSpec(a.shape, lambda b, t: (0, 0))
    return pl.pallas_call(
        functools.partial(_merge_kernel, len(src), n_lat),
        out_shape=[jax.ShapeDtypeStruct((rows_all, D_MODEL), F32),
                   jax.ShapeDtypeStruct((rows_out, D_MODEL), BF16),
                   jax.ShapeDtypeStruct((rows_out, LANES), F32)],
        grid=(n_batch, nt),
        in_specs=_stream_specs(src, nt_all, n_lat) + [seq_blk] * 6
                 + [small(hd), small(gg), small(wo),
                    pl.BlockSpec((1, 1, 6 * D_MODEL), _mod_index(nt_all, n_lat, n_batch)),
                    small(g2), small(wr)],
        out_specs=[pl.BlockSpec((TM, D_MODEL), in_rows),
                   pl.BlockSpec((TM, D_MODEL), out_rows),
                   pl.BlockSpec((TM, LANES), out_rows)],
        scratch_shapes=[pltpu.VMEM((TM, D_MODEL), BF16)],
        input_output_aliases={0: 0} if len(src) == 1 else {},
        compiler_params=_cparams(("parallel", "arbitrary")),
        name="merge_outproj",
    )(*src, *ys, hd, gg, wo, mod, g2, wr)


def _expert_kernel(be_ref, nv_ref, x_ref, w1_ref, w3_ref, w2_ref, o_ref, w1b_ref, w3b_ref, w2b_ref):
    i = pl.program_id(0)

    @pl.when((i == 0) | (be_ref[i] != be_ref[jnp.maximum(i - 1, 0)]))
    def _():
        w1b_ref[...] = w1_ref[0, 0].astype(BF16)
        w3b_ref[...] = w3_ref[0, 0].astype(BF16)
        w2b_ref[...] = w2_ref[0, 0].astype(BF16)

    @pl.when(i < nv_ref[0])
    def _():
        x = x_ref[...]
        a = jnp.dot(x, w1b_ref[...], preferred_element_type=F32)
        b = jnp.dot(x, w3b_ref[...], preferred_element_type=F32)
        h = (a * _sigmoid(a)) * b
        o_ref[...] = jnp.dot(h.astype(BF16), w2b_ref[...], preferred_element_type=F32).astype(BF16)

    @pl.when(i >= nv_ref[0])
    def _():
        o_ref[...] = jnp.zeros_like(o_ref)


def _experts(xs, block_exp, n_valid, w1, w3, w2, layer):
    n_slots = xs.shape[0]
    n_blocks = n_slots // MOE_BM
    w_idx = lambda i, be, nv: (layer, be[i], 0, 0)
    grid_spec = pltpu.PrefetchScalarGridSpec(
        num_scalar_prefetch=2,
        grid=(n_blocks,),
        in_specs=[pl.BlockSpec((MOE_BM, D_MODEL), lambda i, be, nv: (i, 0)),
                  pl.BlockSpec((1, 1, D_MODEL, EXPERT_FF), w_idx),
                  pl.BlockSpec((1, 1, D_MODEL, EXPERT_FF), w_idx),
                  pl.BlockSpec((1, 1, EXPERT_FF, D_MODEL), w_idx)],
        out_specs=pl.BlockSpec((MOE_BM, D_MODEL), lambda i, be, nv: (i, 0)),
        scratch_shapes=[pltpu.VMEM((D_MODEL, EXPERT_FF), BF16),
                        pltpu.VMEM((D_MODEL, EXPERT_FF), BF16),
                        pltpu.VMEM((EXPERT_FF, D_MODEL), BF16)],
    )
    return pl.pallas_call(
        _expert_kernel,
        out_shape=jax.ShapeDtypeStruct((n_slots, D_MODEL), BF16),
        grid_spec=grid_spec,
        compiler_params=_cparams(("arbitrary",)),
        name="moe_experts",
    )(block_exp, n_valid, xs, w1, w3, w2)


def _combine_kernel(x_ref, ya_ref, yb_ref, rt_ref, mod_ref, *rest):
    o_ref = rest[-1]
    gate2 = mod_ref[0, :, 5 * D_MODEL:6 * D_MODEL]
    wa = rt_ref[:, 2:3]
    wb = rt_ref[:, 3:4]
    o_ref[...] = x_ref[...] + gate2 * (wa * ya_ref[...].astype(F32) + wb * yb_ref[...].astype(F32))


def _combine(xx, ya, yb, route, mod, n_batch, nt_all, nt, n_lat, b0, nb, prev_out, in_place):
    in_rows = lambda b, t: ((b + b0) * nt_all + t, 0)
    loc_rows = lambda b, t: (b * nt + t, 0)
    glob_rows = lambda b, t: ((b + b0) * nt + t, 0)
    mod_idx = lambda b, t: (jnp.where(t < n_lat, b + b0, n_batch), 0, 0)
    rows = xx.shape[0] if in_place else n_batch * nt * TM
    operands = [xx, ya, yb, route, mod]
    in_specs = [pl.BlockSpec((TM, D_MODEL), in_rows),
                pl.BlockSpec((TM, D_MODEL), loc_rows),
                pl.BlockSpec((TM, D_MODEL), loc_rows),
                pl.BlockSpec((TM, LANES), glob_rows),
                pl.BlockSpec((1, 1, 6 * D_MODEL), mod_idx)]
    aliases = {0: 0} if in_place else {}
    if prev_out is not None:
        operands.append(prev_out)
        in_specs.append(pl.BlockSpec(memory_space=pl.ANY))
        aliases = {5: 0}
    return pl.pallas_call(
        _combine_kernel,
        out_shape=jax.ShapeDtypeStruct((rows, D_MODEL), F32),
        grid=(nb, nt),
        in_specs=in_specs,
        out_specs=pl.BlockSpec((TM, D_MODEL), in_rows if in_place else glob_rows),
        input_output_aliases=aliases,
        compiler_params=_cparams(("parallel", "arbitrary")),
        name="moe_combine",
    )(*operands)


def _dispatch_plan(eid):
    t = eid.shape[0]
    n_assign = t * TOP_K
    e_flat = eid.reshape(-1)
    experts = jnp.arange(N_EXPERTS, dtype=jnp.int32)
    e_sorted, a_sorted = lax.sort((e_flat, jnp.arange(n_assign, dtype=jnp.int32)), num_keys=1,
                                  is_stable=True)
    counts = jnp.sum((e_flat[:, None] == experts[None, :]).astype(jnp.int32), axis=0)
    padded = (counts + MOE_BM - 1) // MOE_BM * MOE_BM
    pad_end = jnp.cumsum(padded)
    shift = (pad_end - padded) - (jnp.cumsum(counts) - counts)
    n_blocks = -(-n_assign // MOE_BM) + N_EXPERTS
    n_slots = n_blocks * MOE_BM
    blk0 = jnp.arange(n_blocks, dtype=jnp.int32) * MOE_BM
    block_exp = jnp.minimum(jnp.sum((pad_end[None, :] <= blk0[:, None]).astype(jnp.int32), axis=1),
                            N_EXPERTS - 1)
    n_valid = (pad_end[-1] // MOE_BM).astype(jnp.int32).reshape(1)
    block_shift = jnp.sum(jnp.where(block_exp[:, None] == experts[None, :], shift[None, :], 0), axis=1)
    src = jnp.arange(n_slots, dtype=jnp.int32) - jnp.repeat(block_shift, MOE_BM)
    slot_tok = a_sorted.at[jnp.clip(src, 0, n_assign - 1)].get(mode='promise_in_bounds') // TOP_K
    dest = (jnp.arange(n_assign, dtype=jnp.int32)
            + jnp.sum(jnp.where(e_sorted[:, None] == experts[None, :], shift[None, :], 0), axis=1))
    _, pos = lax.sort((a_sorted, dest), num_keys=1)
    return slot_tok, pos.reshape(t, TOP_K), block_exp, n_valid


def _take_rows(a, idx):
    return a.at[idx].get(mode='promise_in_bounds')


def _moe(f, eid, tok0, w1, w3, w2, layer):
    slot_tok, pos, block_exp, n_valid = _dispatch_plan(eid)
    ys = _experts(_take_rows(f, slot_tok + tok0), block_exp, n_valid, w1, w3, w2, layer)
    return _take_rows(ys, pos[:, 0]), _take_rows(ys, pos[:, 1])


def _pack_w_in(w_in):
    o = 0
    cols = {}
    for name, wd in (('cq', 192), ('ckv', 128), ('kr', 32), ('ret', 1024), ('hy', 768), ('lru', 512)):
        cols[name] = w_in[:, o:o + wd]
        o += wd
    swap = jnp.arange(MLA_ROPE) ^ 8
    kr_sw = cols['kr'][:, swap]
    return jnp.concatenate([cols['cq'], cols['kr'], kr_sw, cols['ckv'], cols['ret'], cols['hy'],
                            cols['lru']], axis=1).astype(BF16)


def _pack_mla(w_uq, w_ukv, q_norm_g, kv_norm_g, qn_g, kn_g):
    hw = MLA_HEADS * LANES
    swap = jnp.arange(MLA_ROPE) ^ 8
    eye = jnp.eye(MLA_HEADS, dtype=F32)
    q3 = w_uq.reshape(MLA_Q_RANK, MLA_HEADS, MLA_QK)
    q_main = jnp.pad(q3, ((0, 0), (0, 0), (0, LANES - MLA_QK)))
    q_swap = jnp.pad(q3[:, :, MLA_NOPE + swap], ((0, 0), (0, 0), (MLA_NOPE, LANES - MLA_QK)))
    wq = jnp.concatenate([q_main.reshape(MLA_Q_RANK, hw), q_swap.reshape(MLA_Q_RANK, hw)], axis=1)
    wq = jnp.pad(wq, ((0, 2 * LANES - MLA_Q_RANK), (0, 0)))
    kv3 = w_ukv.reshape(MLA_KV_RANK, MLA_HEADS, MLA_NOPE + MLA_V)
    k_part = jnp.pad(kv3[:, :, :MLA_NOPE], ((0, 0), (0, 0), (0, LANES - MLA_NOPE)))
    v_part = kv3[:, :, None, MLA_NOPE:] * eye[None, :, :, None]
    wkv = jnp.concatenate([k_part.reshape(MLA_KV_RANK, hw),
                           v_part.reshape(MLA_KV_RANK, MLA_HEADS * GROUP_W)], axis=1)
    pad = lambda v, lo, n: jnp.pad(v, (lo, n - lo - v.shape[0])).reshape(1, n)
    gains = (pad(q_norm_g, 0, 2 * LANES), kv_norm_g.reshape(1, LANES),
             pad(qn_g, 0, LANES), pad(qn_g[MLA_NOPE + swap], MLA_NOPE, LANES),
             pad(kn_g[:MLA_NOPE], 0, LANES), pad(kn_g[MLA_NOPE:], MLA_NOPE, LANES),
             pad(kn_g[MLA_NOPE + swap], MLA_NOPE, LANES))
    return wq.astype(BF16), wkv.astype(BF16), gains


def _rope_tables(s_lat, seq):
    rows = s_lat // GRID_W
    row = jnp.repeat(jnp.arange(rows), GRID_W).astype(F32)
    col = jnp.tile(jnp.arange(GRID_W), rows).astype(F32)
    half = MLA_ROPE // 4
    inv_freq = ROPE_BASE ** (-jnp.arange(half, dtype=F32) / half)
    ar = row[:, None] * inv_freq
    ac = col[:, None] * inv_freq
    cos32 = jnp.concatenate([jnp.cos(ar), jnp.cos(ar), jnp.cos(ac), jnp.cos(ac)], axis=1)
    sin32 = jnp.concatenate([-jnp.sin(ar), jnp.sin(ar), -jnp.sin(ac), jnp.sin(ac)], axis=1)
    lanes = ((MLA_NOPE, LANES - MLA_QK),)
    cos_t = jnp.pad(jnp.pad(cos32, ((0, 0),) + lanes, constant_values=1.0),
                    ((0, seq - s_lat), (0, 0)), constant_values=1.0)
    sin_t = jnp.pad(sin32, ((0, seq - s_lat),) + lanes)
    return cos_t, sin_t


def _block_diag(w):
    nb, bw, _ = w.shape
    eye = jnp.eye(nb, dtype=w.dtype)
    return (w[:, :, None, :] * eye[:, None, :, None]).reshape(nb * bw, nb * bw)


def kernel(x, c, ctx, c_ctx, w_mod, b_mod, norm1_g, norm2_g, w_in, mla_q_norm_g, mla_w_uq, mla_kv_norm_g, mla_w_ukv, mla_qn_g, mla_kn_g, ret_log_gamma, ret_norm_g, hy_conv_w, hy_conv_b, hy_w1, hy_b1, hy_w2, hy_b2, hy_w3, hy_freq, hy_d, lru_conv_w, lru_conv_b, lru_wa, lru_ba, lru_wx, lru_bx, lru_lambda, group_norm_g, w_out, moe_w_group, moe_w_expert, moe_w1, moe_w3, moe_w2):
    n_batch, s_lat, d = x.shape
    n_ctx = ctx.shape[1]
    depth = w_mod.shape[0]
    assert d == D_MODEL and n_ctx == TM and s_lat % TM == 0 and s_lat % GRID_W == 0
    n_lat = s_lat // TM
    nt_all = n_lat + 1
    seq = nt_all * TM
    mod_rows = -(-(n_batch + 1) // SUBLANES) * SUBLANES

    cc = jnp.concatenate([c, c_ctx[None, :], jnp.zeros((mod_rows - n_batch - 1, d), F32)], axis=0)
    mod_all = _modulation(cc, w_mod, b_mod)
    src = (x, ctx)
    cos_t, sin_t = _rope_tables(s_lat, seq)

    for l in range(depth):
        ctx_out = l < depth - 1
        nt = nt_all if ctx_out else n_lat
        mod = mod_all[l].reshape(mod_rows, 1, 6 * d)

        zm, zr, zh, zl = _inproj(src, mod, norm1_g[l].reshape(1, d), _pack_w_in(w_in[l]),
                                 n_batch, nt_all, n_lat)
        zm = zm.reshape(n_batch, seq, Z_MLA)
        zr = zr.reshape(n_batch, seq, Z_RET)
        zh = zh.reshape(n_batch, seq, Z_HY)
        zl = zl.reshape(n_batch, seq, Z_LRU)

        wq, wkv, gains = _pack_mla(mla_w_uq[l], mla_w_ukv[l], mla_q_norm_g[l], mla_kv_norm_g[l],
                                   mla_qn_g[l], mla_kn_g[l])
        tq = MLA_TQ if s_lat % MLA_TQ == 0 else TM
        y_mla = _mla(zm, cos_t, sin_t, gains, wq, wkv, n_batch, tq, 0, s_lat // tq, 0, nt_all)
        if ctx_out:
            y_mla = jnp.concatenate(
                [y_mla, _mla(zm, cos_t, sin_t, gains, wq, wkv, n_batch, TM, s_lat, 1, s_lat, 1)], axis=1)

        y_ret = _retention(zr, _retention_tables(ret_log_gamma[l].astype(F32)),
                           ret_norm_g[l].reshape(1, GROUP_W), n_batch, n_lat, ctx_out)

        s_hy, x0_hy = _hy_pre(zh, hy_conv_w[l], hy_conv_b[l].reshape(1, Z_HY), n_batch, n_lat)
        w1p = jnp.pad(hy_w1[l], ((0, LANES - hy_w1.shape[1]), (0, 0)))
        fargs = (w1p, hy_b1[l].reshape(1, -1), hy_w2[l], hy_b2[l].reshape(1, -1), hy_w3[l],
                 hy_freq[l].reshape(1, -1))
        hf, hb = _hy_filter(s_lat, *fargs)
        s_t = jnp.transpose(s_hy[:, :s_lat].reshape(n_batch, n_lat, TM, GROUP_W), (3, 1, 0, 2))
        y_t = _hy_conv(s_t, _toeplitz_rows(hf, hb, n_lat))
        y_hy = jnp.transpose(y_t, (2, 1, 3, 0)).reshape(n_batch, s_lat, GROUP_W)
        if ctx_out:
            hf_c, hb_c = _hy_filter(n_ctx, *fargs)
            sc_t = jnp.transpose(s_hy[:, s_lat:].reshape(n_batch, 1, TM, GROUP_W), (3, 1, 0, 2))
            yc_t = _hy_conv(sc_t, _toeplitz_rows(hf_c, hb_c, 1))
            y_hy_c = jnp.transpose(yc_t, (2, 1, 3, 0)).reshape(n_batch, n_ctx, GROUP_W)
        else:
            y_hy_c = jnp.zeros((n_batch, n_ctx, GROUP_W), BF16)
        y_hy = jnp.concatenate([y_hy, y_hy_c], axis=1)

        wg = jnp.concatenate([_block_diag(lru_wa[l, 0]), _block_diag(lru_wx[l, 0]),
                              _block_diag(lru_wa[l, 1]), _block_diag(lru_wx[l, 1])], axis=1).astype(BF16)
        bg = jnp.concatenate([lru_ba[l, 0], lru_bx[l, 0], lru_ba[l, 1], lru_bx[l, 1]]).reshape(1, -1)
        sp = jax.nn.softplus(-lru_lambda[l].astype(F32))
        y_lru = _lru(zl, lru_conv_w[l], lru_conv_b[l].reshape(1, -1), wg, bg, sp, n_batch, n_lat)

        wr = jnp.pad(jnp.concatenate([moe_w_group[l], moe_w_expert[l]], axis=1),
                     ((0, 0), (0, LANES - MOE_GROUPS - N_EXPERTS))).astype(BF16)
        xx, f, route = _merge(src, (y_mla, y_ret, y_hy, s_hy, x0_hy, y_lru), hy_d[l].reshape(1, -1),
                               group_norm_g[l].reshape(1, -1), w_out[l].astype(BF16), mod,
                               norm2_g[l].reshape(1, d), wr, n_batch, nt_all, nt, n_lat)

        eid = route[:, 0:TOP_K].astype(jnp.int32)
        nb = n_batch // MOE_SPLIT
        th = nb * nt * TM
        parts = [_moe(f, eid[h * th:(h + 1) * th], h * th, moe_w1, moe_w3, moe_w2, l)
                 for h in range(MOE_SPLIT)]
        out = None
        for h, (ya, yb) in enumerate(parts):
            if ctx_out:
                xx = _combine(xx, ya, yb, route, mod, n_batch, nt_all, nt, n_lat, h * nb, nb, None, True)
            else:
                out = _combine(xx, ya, yb, route, mod, n_batch, nt_all, nt, n_lat, h * nb, nb, out, False)
        if not ctx_out:
            xx = out
        src = (xx,)

    return xx.reshape(n_batch, s_lat, d)
```

```python
import functools
import math

import jax
import jax.numpy as jnp
from jax import lax
from jax.experimental import pallas as pl
from jax.experimental.pallas import tpu as pltpu

F32 = jnp.float32
BF16 = jnp.bfloat16

D_MODEL = 1024
EPS = 1e-6
GRID_W = 64
N_GROUPS = 4
GROUP_W = 256

MLA_HEADS = 4
MLA_NOPE = 64
MLA_ROPE = 32
MLA_QK = 96
MLA_V = 64
MLA_Q_RANK = 192
MLA_KV_RANK = 128
ROPE_BASE = 10000.0

RET_HEADS = 4
RET_DK = 64

HY_BANDS = 16
HY_FAST_PCT = 0.3
HY_SLOW_PCT = 1.5
HY_TARGET = 1e-2

LRU_C = 8.0

MOE_GROUPS = 4
MOE_PER_GROUP = 8
N_EXPERTS = 32
TOP_K = 2
EXPERT_FF = 512

LANES = 128
SUBLANES = 8
MXU_DIM = 256

TM = 256
MLA_TQ = 512
MOE_BM = 512
HY_CB = 8
SCAN_UNROLL = 8
RET_UNROLL = 4
VMEM_LIMIT = 56 * 1024 * 1024

Z_MLA, Z_RET, Z_HY, Z_LRU = 384, 1024, 768, 512
Z_COLS = Z_MLA + Z_RET + Z_HY + Z_LRU


def _cparams(sem):
    return pltpu.CompilerParams(dimension_semantics=sem, vmem_limit_bytes=VMEM_LIMIT)


def _bdot(a, b):
    return jnp.dot(a.astype(BF16), b.astype(BF16), preferred_element_type=F32)


def _bdot_t(a, b):
    return lax.dot_general(a.astype(BF16), b.astype(BF16), (((1,), (1,)), ((), ())),
                           preferred_element_type=F32)


def _fdot(a, b):
    return jnp.dot(a, b, preferred_element_type=F32, precision=lax.Precision.HIGHEST)


def _sigmoid(x):
    return 0.5 * jnp.tanh(0.5 * x) + 0.5


def _shift_rows(x, d):
    n = x.shape[0]
    r = pltpu.roll(x, (n - d) % n, 0)
    row = lax.broadcasted_iota(jnp.int32, (n, 1), 0)
    ok = (row + d >= 0) & (row + d < n)
    return jnp.where(ok, r, 0.0)


def _mod_kernel(c_ref, w_ref, b_ref, o_ref):
    c = c_ref[...]
    a = c * jax.nn.sigmoid(c)
    o_ref[0] = _fdot(a, w_ref[0]) + b_ref[0]


def _modulation(cc, w_mod, b_mod):
    nl, d, n6 = w_mod.shape
    rows = cc.shape[0]
    tn = 1536
    return pl.pallas_call(
        _mod_kernel,
        out_shape=jax.ShapeDtypeStruct((nl, rows, n6), F32),
        grid=(nl, n6 // tn),
        in_specs=[pl.BlockSpec((rows, d), lambda l, j: (0, 0)),
                  pl.BlockSpec((1, d, tn), lambda l, j: (l, 0, j)),
                  pl.BlockSpec((1, 1, tn), lambda l, j: (l, 0, j))],
        out_specs=pl.BlockSpec((1, rows, tn), lambda l, j: (l, 0, j)),
        compiler_params=_cparams(("arbitrary", "arbitrary")),
        name="modulation",
    )(cc, w_mod, b_mod.reshape(nl, 1, n6))


def _stream_specs(src, nt_all, n_lat):
    if len(src) == 1:
        return [pl.BlockSpec((TM, D_MODEL), lambda b, t: (b * nt_all + t, 0))]
    return [pl.BlockSpec((1, TM, D_MODEL), lambda b, t: (b, jnp.minimum(t, n_lat - 1), 0)),
            pl.BlockSpec((1, TM, D_MODEL), lambda b, t: (b, 0, 0))]


def _stream_tile(refs, n_lat):
    if len(refs) == 1:
        return refs[0][...]
    return jnp.where(pl.program_id(1) < n_lat, refs[0][0], refs[1][0])


def _inproj_kernel(n_src, n_lat, *refs):
    x = _stream_tile(refs[:n_src], n_lat)
    mod_ref, g_ref, w_ref, zm_ref, zr_ref, zh_ref, zl_ref = refs[n_src:]
    ms = jnp.mean(x * x, axis=-1, keepdims=True)
    y = x * lax.rsqrt(ms + EPS) * g_ref[...]
    sh = mod_ref[0, :, 0:D_MODEL]
    sc = mod_ref[0, :, D_MODEL:2 * D_MODEL]
    a = y * (1.0 + sc) + sh
    z = _bdot(a, w_ref[...])
    o = 0
    for ref, w in ((zm_ref, Z_MLA), (zr_ref, Z_RET), (zh_ref, Z_HY), (zl_ref, Z_LRU)):
        ref[...] = z[:, o:o + w].astype(BF16)
        o += w


def _mod_index(nt_all, n_lat, n_batch):
    def idx(b, t):
        return (jnp.where(t < n_lat, b, n_batch), 0, 0)
    return idx


def _inproj(src, mod, g1, w_in_p, n_batch, nt, n_lat):
    rows = n_batch * nt * TM
    row_map = lambda b, t: (b * nt + t, 0)
    outs = [jax.ShapeDtypeStruct((rows, w), BF16) for w in (Z_MLA, Z_RET, Z_HY, Z_LRU)]
    return pl.pallas_call(
        functools.partial(_inproj_kernel, len(src), n_lat),
        out_shape=outs,
        grid=(n_batch, nt),
        in_specs=_stream_specs(src, nt, n_lat)
                 + [pl.BlockSpec((1, 1, 6 * D_MODEL), _mod_index(nt, n_lat, n_batch)),
                    pl.BlockSpec((1, D_MODEL), lambda b, t: (0, 0)),
                    pl.BlockSpec((D_MODEL, Z_COLS), lambda b, t: (0, 0))],
        out_specs=[pl.BlockSpec((TM, w), row_map) for w in (Z_MLA, Z_RET, Z_HY, Z_LRU)],
        compiler_params=_cparams(("parallel", "arbitrary")),
        name="inproj",
    )(*src, mod, g1, w_in_p)


def _mla_kernel(tq, q_row0, key_row0, n_key_tiles, z_ref, cos_ref, sin_ref, gq_ref, gkv_ref, gqm_ref,
                gqs_ref, gkm_ref, gkr_ref, gks_ref, wq_ref, wkv_ref, o_ref, k_scr, v_scr):
    t = pl.program_id(1)
    hw = MLA_HEADS * LANES
    lane = lax.broadcasted_iota(jnp.int32, (1, LANES), 1)
    is_rope = (lane >= MLA_NOPE) & (lane < MLA_QK)
    vlane = lax.broadcasted_iota(jnp.int32, (1, GROUP_W), 1)

    def kv_tile(i, carry):
        r0 = pl.multiple_of(i * TM, TM)
        rs0 = pl.multiple_of(key_row0 + i * TM, TM)
        z = z_ref[0, pl.ds(rs0, TM), LANES:3 * LANES].astype(F32)
        col1 = z[:, 0:LANES]
        zb = z[:, LANES:2 * LANES]
        cos = cos_ref[pl.ds(rs0, TM), :]
        sin = sin_ref[pl.ds(rs0, TM), :]
        ms_kv = jnp.mean(zb * zb, axis=-1, keepdims=True)
        kv = _bdot(zb * lax.rsqrt(ms_kv + EPS) * gkv_ref[...], wkv_ref[...])
        kr = jnp.where(is_rope, col1, 0.0)
        kr_sw = jnp.where(is_rope, pltpu.roll(col1, LANES - MLA_ROPE, 1), 0.0)
        k_rot = kr * gkr_ref[...] * cos + kr_sw * gks_ref[...] * sin
        ss_kr = jnp.sum(kr * kr, axis=-1, keepdims=True)
        for h in range(MLA_HEADS):
            kn = kv[:, h * LANES:(h + 1) * LANES]
            rk = lax.rsqrt((jnp.sum(kn * kn, axis=-1, keepdims=True) + ss_kr) / MLA_QK + EPS)
            k_scr[pl.ds(r0, TM), h * LANES:(h + 1) * LANES] = (
                rk * (kn * gkm_ref[...] + k_rot)).astype(BF16)
            vh = kv[:, hw + h * GROUP_W:hw + (h + 1) * GROUP_W]
            ones_lane = ((h + 1) % MLA_HEADS) * MLA_V
            v_scr[pl.ds(r0, TM), h * GROUP_W:(h + 1) * GROUP_W] = jnp.where(
                vlane == ones_lane, 1.0, vh).astype(BF16)
        return carry

    @pl.when(t == 0)
    def _():
        lax.fori_loop(0, n_key_tiles, kv_tile, 0)

    q0 = pl.multiple_of(q_row0 + t * tq, TM)
    za = z_ref[0, pl.ds(q0, tq), 0:2 * LANES].astype(F32)
    cos = cos_ref[pl.ds(q0, tq), :]
    sin = sin_ref[pl.ds(q0, tq), :]
    lane2 = lax.broadcasted_iota(jnp.int32, (1, 2 * LANES), 1)
    ms_q = jnp.sum(jnp.where(lane2 < MLA_Q_RANK, za * za, 0.0), axis=-1, keepdims=True) / MLA_Q_RANK
    qall = _bdot(za * lax.rsqrt(ms_q + EPS) * gq_ref[...], wq_ref[...])
    q_heads = []
    for h in range(MLA_HEADS):
        qh = qall[:, h * LANES:(h + 1) * LANES]
        qs = qall[:, hw + h * LANES:hw + (h + 1) * LANES]
        rs = lax.rsqrt(jnp.sum(qh * qh, axis=-1, keepdims=True) / MLA_QK + EPS) * (MLA_QK ** -0.5)
        q_heads.append((rs * (qh * gqm_ref[...] * cos + qs * gqs_ref[...] * sin)).astype(BF16))

    acc = jnp.zeros((tq, GROUP_W), F32)
    for h in range(MLA_HEADS):
        s = _bdot_t(q_heads[h], k_scr[:, h * LANES:(h + 1) * LANES])
        m = jnp.max(s, axis=-1, keepdims=True)
        p = jnp.exp((s - m).astype(BF16))
        pv = jnp.dot(p, v_scr[:, h * GROUP_W:(h + 1) * GROUP_W], preferred_element_type=F32)
        ones_lane = ((h + 1) % MLA_HEADS) * MLA_V
        l = jnp.sum(jnp.where(vlane == ones_lane, pv, 0.0), axis=-1, keepdims=True)
        in_head = (vlane >= h * MLA_V) & (vlane < (h + 1) * MLA_V)
        acc = acc + jnp.where(in_head, pv * (1.0 / l), 0.0)
    o_ref[0] = acc.astype(BF16)


def _mla(zm, cos_t, sin_t, gains, wq, wkv, n_batch, tq, q_row0, n_q_tiles, key_row0, n_key_tiles):
    seq = zm.shape[1]
    hw = MLA_HEADS * LANES
    vw = MLA_HEADS * GROUP_W
    n_keys = n_key_tiles * TM
    small = lambda w: pl.BlockSpec((1, w), lambda b, t: (0, 0))
    return pl.pallas_call(
        functools.partial(_mla_kernel, tq, q_row0, key_row0, n_key_tiles),
        out_shape=jax.ShapeDtypeStruct((n_batch, n_q_tiles * tq, GROUP_W), BF16),
        grid=(n_batch, n_q_tiles),
        in_specs=[pl.BlockSpec((1, seq, Z_MLA), lambda b, t: (b, 0, 0)),
                  pl.BlockSpec((seq, LANES), lambda b, t: (0, 0)),
                  pl.BlockSpec((seq, LANES), lambda b, t: (0, 0)),
                  small(2 * LANES), small(LANES), small(LANES), small(LANES), small(LANES),
                  small(LANES), small(LANES),
                  pl.BlockSpec((2 * LANES, 2 * hw), lambda b, t: (0, 0)),
                  pl.BlockSpec((LANES, hw + vw), lambda b, t: (0, 0))],
        out_specs=pl.BlockSpec((1, tq, GROUP_W), lambda b, t: (b, t, 0)),
        scratch_shapes=[pltpu.VMEM((n_keys, hw), BF16), pltpu.VMEM((n_keys, vw), BF16)],
        compiler_params=_cparams(("parallel", "arbitrary")),
        name="mla_attention",
    )(zm, cos_t, sin_t, *gains, wq, wkv)


def _ret_kernel(n_lat, ctx_out, z_ref, dm_ref, dq_ref, dk_ref, gc_ref, bd_ref, g_ref,
                o_ref, acc_ref, st_ref):
    lane = lax.broadcasted_iota(jnp.int32, (1, GROUP_W), 1)
    hmask = [(lane >= h * RET_DK) & (lane < (h + 1) * RET_DK) for h in range(RET_HEADS)]
    ones_bd = bd_ref[...].astype(BF16)

    def qkv(r0):
        q = z_ref[0, pl.ds(r0, TM), 0:GROUP_W]
        k = z_ref[0, pl.ds(r0, TM), GROUP_W:2 * GROUP_W].astype(F32) * (RET_DK ** -0.5)
        v = z_ref[0, pl.ds(r0, TM), 2 * GROUP_W:3 * GROUP_W]
        return q, k, v

    def inner(q, k, v):
        kb = k.astype(BF16)
        acc = jnp.zeros((TM, GROUP_W), F32)
        for h in range(RET_HEADS):
            qh = jnp.where(hmask[h], q, jnp.zeros_like(q))
            vh = jnp.where(hmask[h], v, jnp.zeros_like(v))
            s = _bdot_t(qh, kb) * dm_ref[h]
            acc = acc + _bdot(s, vh)
        return acc

    def state_update(d, k, v):
        kd = (k * dk_ref[d]).T
        st_ref[...] = st_ref[...] * gc_ref[d] + _bdot(kd, v) * bd_ref[...]

    def cross(d, q):
        return _bdot(q.astype(F32) * dq_ref[d], st_ref[...])

    def finish(r0, o):
        sq = o * o
        hi = sq.astype(BF16)
        lo = (sq - hi.astype(F32)).astype(BF16)
        ms = (jnp.dot(hi, ones_bd, preferred_element_type=F32)
              + jnp.dot(lo, ones_bd, preferred_element_type=F32)) / RET_DK
        gate = z_ref[0, pl.ds(r0, TM), 3 * GROUP_W:4 * GROUP_W].astype(F32)
        y = o * lax.rsqrt(ms + EPS) * g_ref[...] * (gate * _sigmoid(gate))
        o_ref[0, pl.ds(r0, TM), :] = y.astype(BF16)

    c0 = n_lat * TM
    qc, kc, vc = qkv(c0)
    if ctx_out:
        finish(c0, inner(qc, kc, vc))
    else:
        o_ref[0, pl.ds(c0, TM), :] = jnp.zeros((TM, GROUP_W), BF16)

    st_ref[...] = jnp.zeros_like(st_ref)
    state_update(0, kc, vc)

    def fwd(n, carry):
        r0 = pl.multiple_of(n * TM, TM)
        q, k, v = qkv(r0)
        acc_ref[pl.ds(r0, TM), :] = inner(q, k, v) + cross(0, q)
        state_update(0, k, v)
        return carry

    lax.fori_loop(0, n_lat, fwd, 0, unroll=RET_UNROLL)

    st_ref[...] = jnp.zeros_like(st_ref)
    state_update(1, kc, vc)

    def bwd(i, carry):
        n = n_lat - 1 - i
        r0 = pl.multiple_of(n * TM, TM)
        q, k, v = qkv(r0)
        finish(r0, acc_ref[pl.ds(r0, TM), :] + cross(1, q))
        state_update(1, k, v)
        return carry

    lax.fori_loop(0, n_lat, bwd, 0, unroll=RET_UNROLL)


def _retention(zr, tabs, g, n_batch, n_lat, ctx_out):
    seq = zr.shape[1]
    dm, dq, dk, gc, bd = tabs
    full = lambda a: pl.BlockSpec(a.shape, lambda b: (0,) * a.ndim)
    return pl.pallas_call(
        functools.partial(_ret_kernel, n_lat, ctx_out),
        out_shape=jax.ShapeDtypeStruct((n_batch, seq, GROUP_W), BF16),
        grid=(n_batch,),
        in_specs=[pl.BlockSpec((1, seq, Z_RET), lambda b: (b, 0, 0)),
                  full(dm), full(dq), full(dk), full(gc), full(bd), full(g)],
        out_specs=pl.BlockSpec((1, seq, GROUP_W), lambda b: (b, 0, 0)),
        scratch_shapes=[pltpu.VMEM((n_lat * TM, GROUP_W), F32),
                        pltpu.VMEM((GROUP_W, GROUP_W), F32)],
        compiler_params=_cparams(("parallel",)),
        name="retention",
    )(zr, dm, dq, dk, gc, bd, g)


def _retention_tables(lg):
    j = jnp.arange(TM, dtype=F32)
    rel = j[:, None] - j[None, :]
    lf = lg[0][:, None, None]
    lb = lg[1][:, None, None]
    dm = (jnp.where(rel[None] >= 0, jnp.exp(jnp.maximum(rel, 0.0)[None] * lf), 0.0)
          + jnp.where(rel[None] <= 0, jnp.exp(jnp.maximum(-rel, 0.0)[None] * lb), 0.0))
    lane_f = jnp.repeat(lg[0], RET_DK)[None, :]
    lane_b = jnp.repeat(lg[1], RET_DK)[None, :]
    dq = jnp.stack([jnp.exp((j + 1)[:, None] * lane_f), jnp.exp((TM - j)[:, None] * lane_b)])
    dk = jnp.stack([jnp.exp((TM - 1 - j)[:, None] * lane_f), jnp.exp(j[:, None] * lane_b)])
    gc = jnp.stack([jnp.exp(TM * lane_f).reshape(GROUP_W, 1), jnp.exp(TM * lane_b).reshape(GROUP_W, 1)])
    hid = jnp.arange(GROUP_W) // RET_DK
    bd = (hid[:, None] == hid[None, :]).astype(F32)
    return dm, dq, dk, gc, bd


def _hy_pre_kernel(n_lat, z_ref, w_ref, b_ref, s_ref, x0_ref):
    def seg(r0, n):
        z = z_ref[0, r0:r0 + n, :].astype(F32)
        u = (_shift_rows(z, -1) * w_ref[0:1, :] + z * w_ref[1:2, :]
             + _shift_rows(z, 1) * w_ref[2:3, :] + b_ref[...])
        x0_ref[0, r0:r0 + n, :] = u[:, 0:GROUP_W].astype(BF16)
        s_ref[0, r0:r0 + n, :] = (u[:, GROUP_W:2 * GROUP_W] * u[:, 2 * GROUP_W:]).astype(BF16)

    seg(0, n_lat * TM)
    seg(n_lat * TM, z_ref.shape[1] - n_lat * TM)


def _hy_pre(zh, w, b, n_batch, n_lat):
    seq = zh.shape[1]
    return pl.pallas_call(
        functools.partial(_hy_pre_kernel, n_lat),
        out_shape=[jax.ShapeDtypeStruct((n_batch, seq, GROUP_W), BF16)] * 2,
        grid=(n_batch,),
        in_specs=[pl.BlockSpec((1, seq, Z_HY), lambda i: (i, 0, 0)),
                  pl.BlockSpec(w.shape, lambda i: (0, 0)),
                  pl.BlockSpec(b.shape, lambda i: (0, 0))],
        out_specs=[pl.BlockSpec((1, seq, GROUP_W), lambda i: (i, 0, 0))] * 2,
        compiler_params=_cparams(("parallel",)),
        name="hyena_pre",
    )(zh, w, b)


def _hy_filter_kernel(z_ref, win_ref, w1_ref, b1_ref, w2_ref, b2_ref, w3_ref, fr_ref, hf_ref, hb_ref):
    fr = fr_ref[...]
    h = jnp.sin(fr * (_fdot(z_ref[...], w1_ref[...]) + b1_ref[...]))
    h = jnp.sin(fr * (_fdot(h, w2_ref[...]) + b2_ref[...]))
    filt = _fdot(h, w3_ref[...])
    win = win_ref[...]
    hf_ref[...] = filt[:, 0:GROUP_W] * win
    hb_ref[...] = filt[:, GROUP_W:] * win


def _hy_filter(n, w1p, b1, w2, b2, w3, freq):
    t = jnp.linspace(0.0, 1.0, n, dtype=F32)[:, None]
    bands = jnp.linspace(1e-4, HY_BANDS - 1, HY_BANDS, dtype=F32)
    w = 2.0 * math.pi * jnp.arange(n, dtype=F32)[:, None] / n
    z = jnp.concatenate([t, jnp.cos(bands * w), -jnp.sin(bands * w)], axis=-1)
    z = jnp.pad(z, ((0, 0), (0, LANES - z.shape[1])))
    max_decay = math.log(HY_TARGET) / HY_FAST_PCT
    min_decay = math.log(HY_TARGET) / HY_SLOW_PCT
    deltas = jnp.abs(jnp.linspace(min_decay, max_decay, GROUP_W, dtype=F32))
    window = jnp.exp(-t * deltas)
    full = lambda a: pl.BlockSpec(a.shape, lambda i: (0, 0))
    rows = lambda wd: pl.BlockSpec((TM, wd), lambda i: (i, 0))
    return pl.pallas_call(
        _hy_filter_kernel,
        out_shape=[jax.ShapeDtypeStruct((n, GROUP_W), F32)] * 2,
        grid=(n // TM,),
        in_specs=[rows(LANES), rows(GROUP_W), full(w1p), full(b1), full(w2), full(b2), full(w3),
                  full(freq)],
        out_specs=[rows(GROUP_W)] * 2,
        compiler_params=_cparams(("arbitrary",)),
        name="hyena_filter",
    )(z, window, w1p, b1, w2, b2, w3, freq)


def _toeplitz_rows(hf, hb, nj):
    n = hf.shape[0]
    full = jnp.concatenate([hf, jnp.zeros((1, GROUP_W), F32), hb[1:][::-1]], axis=0)
    d = jnp.arange(-(nj - 1), nj)[:, None]
    m = jnp.arange(2 * TM)[None, :]
    lag = d * TM + jnp.where(m < TM, m, m - 2 * TM)
    idx = jnp.mod(lag, 2 * n)
    return jnp.transpose(full[idx], (2, 0, 1))


def _hy_conv_kernel(nj, nb, s_ref, w_ref, y_ref, acc_ref):
    def chan(c, carry):
        sc = s_ref[c].reshape(nj * nb, TM)
        acc_ref[...] = jnp.zeros_like(acc_ref)
        for di in range(2 * nj - 1):
            d = di - (nj - 1)
            w = w_ref[c, di:di + 1, :]
            rolled = pltpu.roll(jnp.broadcast_to(w, (TM, 2 * TM)), 0, 1, stride=1, stride_axis=0)
            toep = rolled[:, 0:TM].astype(BF16)
            if d >= 0:
                res = jnp.dot(sc[0:(nj - d) * nb], toep, preferred_element_type=F32)
                acc_ref[d * nb:nj * nb, :] += res
            else:
                res = jnp.dot(sc[(-d) * nb:nj * nb], toep, preferred_element_type=F32)
                acc_ref[0:(nj + d) * nb, :] += res
        y_ref[c] = acc_ref[...].reshape(nj, nb, TM).astype(BF16)
        return carry

    lax.fori_loop(0, HY_CB, chan, 0)


def _hy_conv(s_t, w_rows):
    ch, nj, nb, _ = s_t.shape
    nd = 2 * nj - 1
    return pl.pallas_call(
        functools.partial(_hy_conv_kernel, nj, nb),
        out_shape=jax.ShapeDtypeStruct(s_t.shape, BF16),
        grid=(ch // HY_CB,),
        in_specs=[pl.BlockSpec((HY_CB, nj, nb, TM), lambda i: (i, 0, 0, 0)),
                  pl.BlockSpec((HY_CB, nd, 2 * TM), lambda i: (i, 0, 0))],
        out_specs=pl.BlockSpec((HY_CB, nj, nb, TM), lambda i: (i, 0, 0, 0)),
        scratch_shapes=[pltpu.VMEM((nj * nb, TM), F32)],
        compiler_params=_cparams(("parallel",)),
        name="hyena_conv",
    )(s_t, w_rows)


def _lru_tile_scan(a, b, carry, reverse):
    row = lax.broadcasted_iota(jnp.int32, (SUBLANES, 1), 0)
    for k in (1, 2, 4):
        if reverse:
            ok = row < SUBLANES - k
            sh = SUBLANES - k
        else:
            ok = row >= k
            sh = k
        a_s = jnp.where(ok, pltpu.roll(a, sh, 0), 1.0)
        b_s = jnp.where(ok, pltpu.roll(b, sh, 0), 0.0)
        b = a * b_s + b
        a = a * a_s
    return b + a * carry


def _lru_kernel(n_lat, z_ref, cw_ref, cb_ref, wg_ref, bg_ref, sp_ref, o_ref,
                af_ref, bf_ref, ab_ref, bb_ref, h_ref):
    seq = z_ref.shape[1]
    s_lat = n_lat * TM

    def prep(r0, n):
        x = z_ref[0, r0:r0 + n, 0:GROUP_W].astype(F32)
        x = (_shift_rows(x, -2) * cw_ref[0:1, :] + _shift_rows(x, -1) * cw_ref[1:2, :]
             + x * cw_ref[2:3, :] + _shift_rows(x, 1) * cw_ref[3:4, :] + cb_ref[...])
        g = _bdot(x, wg_ref[...]) + bg_ref[...]
        for d, (a_ref, b_ref) in enumerate(((af_ref, bf_ref), (ab_ref, bb_ref))):
            r = _sigmoid(g[:, (2 * d) * GROUP_W:(2 * d + 1) * GROUP_W])
            i = _sigmoid(g[:, (2 * d + 1) * GROUP_W:(2 * d + 2) * GROUP_W])
            a = jnp.exp(-LRU_C * r * sp_ref[d:d + 1, :])
            a_ref[r0:r0 + n, :] = a
            b_ref[r0:r0 + n, :] = jnp.sqrt(1.0 - a * a) * (i * x)

    prep(0, s_lat)
    prep(s_lat, seq - s_lat)

    n_ctx_tiles = (seq - s_lat) // SUBLANES
    n_lat_tiles = s_lat // SUBLANES

    def fwd_tile(base):
        def body(i, carry):
            r0 = pl.multiple_of(base + i * SUBLANES, SUBLANES)
            h = _lru_tile_scan(af_ref[pl.ds(r0, SUBLANES), :], bf_ref[pl.ds(r0, SUBLANES), :],
                               carry, False)
            h_ref[pl.ds(r0, SUBLANES), :] = h
            return h[SUBLANES - 1:SUBLANES, :]
        return body

    carry = jnp.zeros((1, GROUP_W), F32)
    carry = lax.fori_loop(0, n_ctx_tiles, fwd_tile(s_lat), carry, unroll=SCAN_UNROLL)
    lax.fori_loop(0, n_lat_tiles, fwd_tile(0), carry, unroll=SCAN_UNROLL)

    def bwd_tile(base, n_tiles):
        def body(i, carry):
            r0 = pl.multiple_of(base + (n_tiles - 1 - i) * SUBLANES, SUBLANES)
            h = _lru_tile_scan(ab_ref[pl.ds(r0, SUBLANES), :], bb_ref[pl.ds(r0, SUBLANES), :],
                               carry, True)
            gate = z_ref[0, pl.ds(r0, SUBLANES), GROUP_W:2 * GROUP_W].astype(F32)
            y = (h + h_ref[pl.ds(r0, SUBLANES), :]) * jax.nn.gelu(gate, approximate=True)
            o_ref[0, pl.ds(r0, SUBLANES), :] = y.astype(BF16)
            return h[0:1, :]
        return body

    carry = jnp.zeros((1, GROUP_W), F32)
    carry = lax.fori_loop(0, n_ctx_tiles, bwd_tile(s_lat, n_ctx_tiles), carry, unroll=SCAN_UNROLL)
    lax.fori_loop(0, n_lat_tiles, bwd_tile(0, n_lat_tiles), carry, unroll=SCAN_UNROLL)


def _lru(zl, cw, cb, wg, bg, sp, n_batch, n_lat):
    seq = zl.shape[1]
    full = lambda a: pl.BlockSpec(a.shape, lambda i: (0,) * a.ndim)
    return pl.pallas_call(
        functools.partial(_lru_kernel, n_lat),
        out_shape=jax.ShapeDtypeStruct((n_batch, seq, GROUP_W), BF16),
        grid=(n_batch,),
        in_specs=[pl.BlockSpec((1, seq, Z_LRU), lambda i: (i, 0, 0)),
                  full(cw), full(cb), full(wg), full(bg), full(sp)],
        out_specs=pl.BlockSpec((1, seq, GROUP_W), lambda i: (i, 0, 0)),
        scratch_shapes=[pltpu.VMEM((seq, GROUP_W), F32)] * 5,
        compiler_params=_cparams(("parallel",)),
        name="rglru",
    )(zl, cw, cb, wg, bg, sp)


def _merge_kernel(n_src, n_lat, *refs):
    x_in = _stream_tile(refs[:n_src], n_lat)
    (ya_ref, yb_ref, yc_ref, s_ref, x0_ref, yd_ref, hd_ref, gg_ref, wo_ref,
     mod_ref, g2_ref, wr_ref, tri_ref, x1_ref, f_ref, rt_ref, hist_ref, cat_ref) = refs[n_src:]
    s = s_ref[0].astype(F32)
    yh = x0_ref[0].astype(F32) * (yc_ref[0].astype(F32) + s * hd_ref[...])
    parts = (ya_ref[0].astype(F32), yb_ref[0].astype(F32), yh, yd_ref[0].astype(F32))
    for g, y in enumerate(parts):
        ms = jnp.mean(y * y, axis=-1, keepdims=True)
        gg = gg_ref[:, g * GROUP_W:(g + 1) * GROUP_W]
        cat_ref[:, g * GROUP_W:(g + 1) * GROUP_W] = (y * lax.rsqrt(ms + EPS) * gg).astype(BF16)
    m = jnp.dot(cat_ref[...], wo_ref[...], preferred_element_type=F32)
    gate1 = mod_ref[0, :, 2 * D_MODEL:3 * D_MODEL]
    x1 = x_in + gate1 * m
    x1_ref[...] = x1
    ms = jnp.mean(x1 * x1, axis=-1, keepdims=True)
    sh = mod_ref[0, :, 3 * D_MODEL:4 * D_MODEL]
    sc = mod_ref[0, :, 4 * D_MODEL:5 * D_MODEL]
    f = (x1 * lax.rsqrt(ms + EPS) * g2_ref[...]) * (1.0 + sc) + sh
    fb = f.astype(BF16)
    f_ref[...] = fb
    route, hist = _route(jnp.dot(fb, wr_ref[...], preferred_element_type=F32), tri_ref[...])
    rt_ref[...] = route
    hist_ref[0] = jnp.broadcast_to(hist, (SUBLANES, LANES))


def _route(logits, tri):
    lane = lax.broadcasted_iota(jnp.int32, logits.shape, 1)
    neg = jnp.float32(-jnp.inf)
    big = jnp.int32(LANES)
    gl = jnp.where(lane < MOE_GROUPS, logits, neg)
    gmax = jnp.max(gl, axis=-1, keepdims=True)
    gsum = jnp.sum(jnp.exp(gl - gmax), axis=-1, keepdims=True)
    g_val = 1.0 / gsum
    g_idx = jnp.min(jnp.where(gl == gmax, lane, big), axis=-1, keepdims=True)
    lo = MOE_GROUPS + MOE_PER_GROUP * g_idx
    el = jnp.where((lane >= lo) & (lane < lo + MOE_PER_GROUP), logits, neg)
    m1 = jnp.max(el, axis=-1, keepdims=True)
    i1 = jnp.min(jnp.where(el == m1, lane, big), axis=-1, keepdims=True)
    el2 = jnp.where(lane == i1, neg, el)
    m2 = jnp.max(el2, axis=-1, keepdims=True)
    i2 = jnp.min(jnp.where(el2 == m2, lane, big), axis=-1, keepdims=True)
    r = jnp.exp(m2 - m1)
    w1 = g_val / (1.0 + r)
    w2 = w1 * r
    is1 = lane == i1 - MOE_GROUPS
    is2 = lane == i2 - MOE_GROUPS
    onehot = jnp.where(is1 | is2, 1.0, 0.0)
    before = jnp.dot(tri, onehot.astype(BF16), preferred_element_type=F32)
    r1 = jnp.sum(jnp.where(is1, before, 0.0), axis=-1, keepdims=True)
    r2 = jnp.sum(jnp.where(is2, before, 0.0), axis=-1, keepdims=True)
    hist = jnp.sum(onehot, axis=0, keepdims=True)
    e1 = (i1 - MOE_GROUPS).astype(F32)
    e2 = (i2 - MOE_GROUPS).astype(F32)
    vals = (e1, e2, w1, w2, r1, r2)
    route = jnp.zeros(logits.shape, F32)
    for k, v in enumerate(vals):
        route = jnp.where(lane == k, v, route)
    return route, hist


def _merge(src, ys, hd, gg, wo, mod, g2, wr, n_batch, nt_all, nt, n_lat):
    rows_all = n_batch * nt_all * TM
    rows_out = n_batch * nt * TM
    in_rows = lambda b, t: (b * nt_all + t, 0)
    out_rows = lambda b, t: (b * nt + t, 0)
    seq_blk = pl.BlockSpec((1, TM, GROUP_W), lambda b, t: (b, t, 0))
    small = lambda a: pl.BlockSpec(a.shape, lambda b, t: (0, 0))
    row_id = jnp.arange(TM)
    tri = (row_id[None, :] < row_id[:, None]).astype(BF16)
    return pl.pallas_call(
        functools.partial(_merge_kernel, len(src), n_lat),
        out_shape=[jax.ShapeDtypeStruct((rows_all, D_MODEL), F32),
                   jax.ShapeDtypeStruct((rows_out, D_MODEL), BF16),
                   jax.ShapeDtypeStruct((rows_out, LANES), F32),
                   jax.ShapeDtypeStruct((n_batch * nt, SUBLANES, LANES), F32)],
        grid=(n_batch, nt),
        in_specs=_stream_specs(src, nt_all, n_lat) + [seq_blk] * 6
                 + [small(hd), small(gg), small(wo),
                    pl.BlockSpec((1, 1, 6 * D_MODEL), _mod_index(nt_all, n_lat, n_batch)),
                    small(g2), small(wr), small(tri)],
        out_specs=[pl.BlockSpec((TM, D_MODEL), in_rows),
                   pl.BlockSpec((TM, D_MODEL), out_rows),
                   pl.BlockSpec((TM, LANES), out_rows),
                   pl.BlockSpec((1, SUBLANES, LANES), lambda b, t: (b * nt + t, 0, 0))],
        scratch_shapes=[pltpu.VMEM((TM, D_MODEL), BF16)],
        input_output_aliases={0: 0} if len(src) == 1 else {},
        compiler_params=_cparams(("parallel", "arbitrary")),
        name="merge_outproj",
    )(*src, *ys, hd, gg, wo, mod, g2, wr, tri)


def _expert_kernel(be_ref, nv_ref, x_ref, w1_ref, w3_ref, w2_ref, o_ref, w1b_ref, w3b_ref, w2b_ref):
    i = pl.program_id(0)

    @pl.when((i == 0) | (be_ref[i] != be_ref[jnp.maximum(i - 1, 0)]))
    def _():
        w1b_ref[...] = w1_ref[0, 0].astype(BF16)
        w3b_ref[...] = w3_ref[0, 0].astype(BF16)
        w2b_ref[...] = w2_ref[0, 0].astype(BF16)

    @pl.when(i < nv_ref[0])
    def _():
        x = x_ref[...]
        a = jnp.dot(x, w1b_ref[...], preferred_element_type=F32)
        b = jnp.dot(x, w3b_ref[...], preferred_element_type=F32)
        h = (a * _sigmoid(a)) * b
        o_ref[...] = jnp.dot(h.astype(BF16), w2b_ref[...], preferred_element_type=F32).astype(BF16)

    @pl.when(i >= nv_ref[0])
    def _():
        o_ref[...] = jnp.zeros_like(o_ref)


def _experts(xs, block_exp, n_valid, w1, w3, w2, layer):
    n_slots = xs.shape[0]
    n_blocks = n_slots // MOE_BM
    w_idx = lambda i, be, nv: (layer, be[i], 0, 0)
    grid_spec = pltpu.PrefetchScalarGridSpec(
        num_scalar_prefetch=2,
        grid=(n_blocks,),
        in_specs=[pl.BlockSpec((MOE_BM, D_MODEL), lambda i, be, nv: (i, 0)),
                  pl.BlockSpec((1, 1, D_MODEL, EXPERT_FF), w_idx),
                  pl.BlockSpec((1, 1, D_MODEL, EXPERT_FF), w_idx),
                  pl.BlockSpec((1, 1, EXPERT_FF, D_MODEL), w_idx)],
        out_specs=pl.BlockSpec((MOE_BM, D_MODEL), lambda i, be, nv: (i, 0)),
        scratch_shapes=[pltpu.VMEM((D_MODEL, EXPERT_FF), BF16),
                        pltpu.VMEM((D_MODEL, EXPERT_FF), BF16),
                        pltpu.VMEM((EXPERT_FF, D_MODEL), BF16)],
    )
    return pl.pallas_call(
        _expert_kernel,
        out_shape=jax.ShapeDtypeStruct((n_slots, D_MODEL), BF16),
        grid_spec=grid_spec,
        compiler_params=_cparams(("arbitrary",)),
        name="moe_experts",
    )(block_exp, n_valid, xs, w1, w3, w2)


def _combine_kernel(x_ref, y_ref, rt_ref, mod_ref, o_ref):
    gate2 = mod_ref[0, :, 5 * D_MODEL:6 * D_MODEL]
    ya = y_ref[:, 0:D_MODEL].astype(F32)
    yb = y_ref[:, D_MODEL:2 * D_MODEL].astype(F32)
    o_ref[...] = x_ref[...] + gate2 * (rt_ref[:, 2:3] * ya + rt_ref[:, 3:4] * yb)


def _combine(xx, yab, route, mod, n_batch, nt_all, nt, n_lat, in_place):
    in_rows = lambda b, t: (b * nt_all + t, 0)
    out_rows = lambda b, t: (b * nt + t, 0)
    rows = xx.shape[0] if in_place else n_batch * nt * TM
    return pl.pallas_call(
        _combine_kernel,
        out_shape=jax.ShapeDtypeStruct((rows, D_MODEL), F32),
        grid=(n_batch, nt),
        in_specs=[pl.BlockSpec((TM, D_MODEL), in_rows),
                  pl.BlockSpec((TM, TOP_K * D_MODEL), out_rows),
                  pl.BlockSpec((TM, LANES), out_rows),
                  pl.BlockSpec((1, 1, 6 * D_MODEL), _mod_index(nt_all, n_lat, n_batch))],
        out_specs=pl.BlockSpec((TM, D_MODEL), in_rows if in_place else out_rows),
        input_output_aliases={0: 0} if in_place else {},
        compiler_params=_cparams(("parallel", "arbitrary")),
        name="moe_combine",
    )(xx, yab, route, mod)


def _dispatch_plan(route, hist):
    t = route.shape[0]
    n_assign = t * TOP_K
    key_bits = (n_assign - 1).bit_length()
    assert key_bits + (N_EXPERTS - 1).bit_length() <= 31
    eid = route[:, 0:TOP_K].astype(jnp.int32)
    rank = route[:, 2 * TOP_K:3 * TOP_K].astype(jnp.int32)
    experts = jnp.arange(N_EXPERTS, dtype=jnp.int32)
    keys = (eid.reshape(-1) << key_bits) | jnp.arange(n_assign, dtype=jnp.int32)
    a_sorted = lax.sort(keys, is_stable=False) & ((1 << key_bits) - 1)
    tile_hist = hist[:, 0, :N_EXPERTS].astype(jnp.int32)
    counts = jnp.sum(tile_hist, axis=0)
    padded = (counts + MOE_BM - 1) // MOE_BM * MOE_BM
    pad_end = jnp.cumsum(padded)
    pad_start = pad_end - padded
    shift = pad_start - (jnp.cumsum(counts) - counts)
    tile_off = pad_start[None, :] + jnp.cumsum(tile_hist, axis=0) - tile_hist
    tok_off = jnp.repeat(tile_off, TM, axis=0)
    pos = jnp.stack([jnp.sum(jnp.where(eid[:, k:k + 1] == experts[None, :], tok_off, 0), axis=1)
                     + rank[:, k] for k in range(TOP_K)], axis=1)
    n_blocks = -(-n_assign // MOE_BM) + N_EXPERTS
    n_slots = n_blocks * MOE_BM
    blk0 = jnp.arange(n_blocks, dtype=jnp.int32) * MOE_BM
    block_exp = jnp.minimum(jnp.sum((pad_end[None, :] <= blk0[:, None]).astype(jnp.int32), axis=1),
                            N_EXPERTS - 1)
    n_valid = (pad_end[-1] // MOE_BM).astype(jnp.int32).reshape(1)
    block_shift = jnp.sum(jnp.where(block_exp[:, None] == experts[None, :], shift[None, :], 0), axis=1)
    src = jnp.arange(n_slots, dtype=jnp.int32) - jnp.repeat(block_shift, MOE_BM)
    slot_tok = a_sorted.at[jnp.clip(src, 0, n_assign - 1)].get(mode='promise_in_bounds') // TOP_K
    return slot_tok, pos, block_exp, n_valid


def _take_rows(a, idx):
    return a.at[idx].get(mode='promise_in_bounds')


def _moe(f, route, hist, w1, w3, w2, layer):
    slot_tok, pos, block_exp, n_valid = _dispatch_plan(route, hist)
    ys = _experts(_take_rows(f, slot_tok), block_exp, n_valid, w1, w3, w2, layer)
    return _take_rows(ys, pos.reshape(-1)).reshape(f.shape[0], TOP_K * D_MODEL)


def _pack_w_in(w_in):
    o = 0
    cols = {}
    for name, wd in (('cq', 192), ('ckv', 128), ('kr', 32), ('ret', 1024), ('hy', 768), ('lru', 512)):
        cols[name] = w_in[:, o:o + wd]
        o += wd
    swap = jnp.arange(MLA_ROPE) ^ 8
    kr_sw = cols['kr'][:, swap]
    return jnp.concatenate([cols['cq'], cols['kr'], kr_sw, cols['ckv'], cols['ret'], cols['hy'],
                            cols['lru']], axis=1).astype(BF16)


def _pack_mla(w_uq, w_ukv, q_norm_g, kv_norm_g, qn_g, kn_g):
    hw = MLA_HEADS * LANES
    swap = jnp.arange(MLA_ROPE) ^ 8
    eye = jnp.eye(MLA_HEADS, dtype=F32)
    q3 = w_uq.reshape(MLA_Q_RANK, MLA_HEADS, MLA_QK)
    q_main = jnp.pad(q3, ((0, 0), (0, 0), (0, LANES - MLA_QK)))
    q_swap = jnp.pad(q3[:, :, MLA_NOPE + swap], ((0, 0), (0, 0), (MLA_NOPE, LANES - MLA_QK)))
    wq = jnp.concatenate([q_main.reshape(MLA_Q_RANK, hw), q_swap.reshape(MLA_Q_RANK, hw)], axis=1)
    wq = jnp.pad(wq, ((0, 2 * LANES - MLA_Q_RANK), (0, 0)))
    kv3 = w_ukv.reshape(MLA_KV_RANK, MLA_HEADS, MLA_NOPE + MLA_V)
    k_part = jnp.pad(kv3[:, :, :MLA_NOPE], ((0, 0), (0, 0), (0, LANES - MLA_NOPE)))
    v_part = kv3[:, :, None, MLA_NOPE:] * eye[None, :, :, None]
    wkv = jnp.concatenate([k_part.reshape(MLA_KV_RANK, hw),
                           v_part.reshape(MLA_KV_RANK, MLA_HEADS * GROUP_W)], axis=1)
    pad = lambda v, lo, n: jnp.pad(v, (lo, n - lo - v.shape[0])).reshape(1, n)
    gains = (pad(q_norm_g, 0, 2 * LANES), kv_norm_g.reshape(1, LANES),
             pad(qn_g, 0, LANES), pad(qn_g[MLA_NOPE + swap], MLA_NOPE, LANES),
             pad(kn_g[:MLA_NOPE], 0, LANES), pad(kn_g[MLA_NOPE:], MLA_NOPE, LANES),
             pad(kn_g[MLA_NOPE + swap], MLA_NOPE, LANES))
    return wq.astype(BF16), wkv.astype(BF16), gains


def _rope_tables(s_lat, seq):
    rows = s_lat // GRID_W
    row = jnp.repeat(jnp.arange(rows), GRID_W).astype(F32)
    col = jnp.tile(jnp.arange(GRID_W), rows).astype(F32)
    half = MLA_ROPE // 4
    inv_freq = ROPE_BASE ** (-jnp.arange(half, dtype=F32) / half)
    ar = row[:, None] * inv_freq
    ac = col[:, None] * inv_freq
    cos32 = jnp.concatenate([jnp.cos(ar), jnp.cos(ar), jnp.cos(ac), jnp.cos(ac)], axis=1)
    sin32 = jnp.concatenate([-jnp.sin(ar), jnp.sin(ar), -jnp.sin(ac), jnp.sin(ac)], axis=1)
    lanes = ((MLA_NOPE, LANES - MLA_QK),)
    cos_t = jnp.pad(jnp.pad(cos32, ((0, 0),) + lanes, constant_values=1.0),
                    ((0, seq - s_lat), (0, 0)), constant_values=1.0)
    sin_t = jnp.pad(sin32, ((0, seq - s_lat),) + lanes)
    return cos_t, sin_t


def _block_diag(w):
    nb, bw, _ = w.shape
    eye = jnp.eye(nb, dtype=w.dtype)
    return (w[:, :, None, :] * eye[:, None, :, None]).reshape(nb * bw, nb * bw)


def kernel(x, c, ctx, c_ctx, w_mod, b_mod, norm1_g, norm2_g, w_in, mla_q_norm_g, mla_w_uq, mla_kv_norm_g, mla_w_ukv, mla_qn_g, mla_kn_g, ret_log_gamma, ret_norm_g, hy_conv_w, hy_conv_b, hy_w1, hy_b1, hy_w2, hy_b2, hy_w3, hy_freq, hy_d, lru_conv_w, lru_conv_b, lru_wa, lru_ba, lru_wx, lru_bx, lru_lambda, group_norm_g, w_out, moe_w_group, moe_w_expert, moe_w1, moe_w3, moe_w2):
    n_batch, s_lat, d = x.shape
    n_ctx = ctx.shape[1]
    depth = w_mod.shape[0]
    assert d == D_MODEL and n_ctx == TM and s_lat % TM == 0 and s_lat % GRID_W == 0
    n_lat = s_lat // TM
    nt_all = n_lat + 1
    seq = nt_all * TM
    mod_rows = -(-(n_batch + 1) // SUBLANES) * SUBLANES

    cc = jnp.concatenate([c, c_ctx[None, :], jnp.zeros((mod_rows - n_batch - 1, d), F32)], axis=0)
    mod_all = _modulation(cc, w_mod, b_mod)
    src = (x, ctx)
    cos_t, sin_t = _rope_tables(s_lat, seq)

    for l in range(depth):
        ctx_out = l < depth - 1
        nt = nt_all if ctx_out else n_lat
        mod = mod_all[l].reshape(mod_rows, 1, 6 * d)

        zm, zr, zh, zl = _inproj(src, mod, norm1_g[l].reshape(1, d), _pack_w_in(w_in[l]),
                                 n_batch, nt_all, n_lat)
        zm = zm.reshape(n_batch, seq, Z_MLA)
        zr = zr.reshape(n_batch, seq, Z_RET)
        zh = zh.reshape(n_batch, seq, Z_HY)
        zl = zl.reshape(n_batch, seq, Z_LRU)

        wq, wkv, gains = _pack_mla(mla_w_uq[l], mla_w_ukv[l], mla_q_norm_g[l], mla_kv_norm_g[l],
                                   mla_qn_g[l], mla_kn_g[l])
        tq = MLA_TQ if s_lat % MLA_TQ == 0 else TM
        y_mla = _mla(zm, cos_t, sin_t, gains, wq, wkv, n_batch, tq, 0, s_lat // tq, 0, nt_all)
        if ctx_out:
            y_mla = jnp.concatenate(
                [y_mla, _mla(zm, cos_t, sin_t, gains, wq, wkv, n_batch, TM, s_lat, 1, s_lat, 1)], axis=1)

        y_ret = _retention(zr, _retention_tables(ret_log_gamma[l].astype(F32)),
                           ret_norm_g[l].reshape(1, GROUP_W), n_batch, n_lat, ctx_out)

        s_hy, x0_hy = _hy_pre(zh, hy_conv_w[l], hy_conv_b[l].reshape(1, Z_HY), n_batch, n_lat)
        w1p = jnp.pad(hy_w1[l], ((0, LANES - hy_w1.shape[1]), (0, 0)))
        fargs = (w1p, hy_b1[l].reshape(1, -1), hy_w2[l], hy_b2[l].reshape(1, -1), hy_w3[l],
                 hy_freq[l].reshape(1, -1))
        hf, hb = _hy_filter(s_lat, *fargs)
        s_t = jnp.transpose(s_hy[:, :s_lat].reshape(n_batch, n_lat, TM, GROUP_W), (3, 1, 0, 2))
        y_t = _hy_conv(s_t, _toeplitz_rows(hf, hb, n_lat))
        y_hy = jnp.transpose(y_t, (2, 1, 3, 0)).reshape(n_batch, s_lat, GROUP_W)
        if ctx_out:
            hf_c, hb_c = _hy_filter(n_ctx, *fargs)
            sc_t = jnp.transpose(s_hy[:, s_lat:].reshape(n_batch, 1, TM, GROUP_W), (3, 1, 0, 2))
            yc_t = _hy_conv(sc_t, _toeplitz_rows(hf_c, hb_c, 1))
            y_hy_c = jnp.transpose(yc_t, (2, 1, 3, 0)).reshape(n_batch, n_ctx, GROUP_W)
        else:
            y_hy_c = jnp.zeros((n_batch, n_ctx, GROUP_W), BF16)
        y_hy = jnp.concatenate([y_hy, y_hy_c], axis=1)

        wg = jnp.concatenate([_block_diag(lru_wa[l, 0]), _block_diag(lru_wx[l, 0]),
                              _block_diag(lru_wa[l, 1]), _block_diag(lru_wx[l, 1])], axis=1).astype(BF16)
        bg = jnp.concatenate([lru_ba[l, 0], lru_bx[l, 0], lru_ba[l, 1], lru_bx[l, 1]]).reshape(1, -1)
        sp = jax.nn.softplus(-lru_lambda[l].astype(F32))
        y_lru = _lru(zl, lru_conv_w[l], lru_conv_b[l].reshape(1, -1), wg, bg, sp, n_batch, n_lat)

        wr = jnp.pad(jnp.concatenate([moe_w_group[l], moe_w_expert[l]], axis=1),
                     ((0, 0), (0, LANES - MOE_GROUPS - N_EXPERTS))).astype(BF16)
        xx, f, route, hist = _merge(src, (y_mla, y_ret, y_hy, s_hy, x0_hy, y_lru), hy_d[l].reshape(1, -1),
                                     group_norm_g[l].reshape(1, -1), w_out[l].astype(BF16), mod,
                                     norm2_g[l].reshape(1, d), wr, n_batch, nt_all, nt, n_lat)

        yab = _moe(f, route, hist, moe_w1, moe_w3, moe_w2, l)
        xx = _combine(xx, yab, route, mod, n_batch, nt_all, nt, n_lat, in_place=ctx_out)
        src = (xx,)

    return xx.reshape(n_batch, s_lat, d)
```

```python
import functools
import math

import jax
import jax.numpy as jnp
from jax import lax
from jax.experimental import pallas as pl
from jax.experimental.pallas import tpu as pltpu

F32 = jnp.float32
BF16 = jnp.bfloat16

D_MODEL = 1024
EPS = 1e-6
GRID_W = 64
N_GROUPS = 4
GROUP_W = 256

MLA_HEADS = 4
MLA_NOPE = 64
MLA_ROPE = 32
MLA_QK = 96
MLA_V = 64
MLA_Q_RANK = 192
MLA_KV_RANK = 128
ROPE_BASE = 10000.0

RET_HEADS = 4
RET_DK = 64

HY_BANDS = 16
HY_FAST_PCT = 0.3
HY_SLOW_PCT = 1.5
HY_TARGET = 1e-2

LRU_C = 8.0

MOE_GROUPS = 4
MOE_PER_GROUP = 8
N_EXPERTS = 32
TOP_K = 2
EXPERT_FF = 512

LANES = 128
SUBLANES = 8
MXU_DIM = 256

TM = 256
MLA_TQ = 512
MOE_BM = 512
HY_CB = 8
SCAN_UNROLL = 8
RET_UNROLL = 4
VMEM_LIMIT = 56 * 1024 * 1024

Z_MLA, Z_RET, Z_HY, Z_LRU = 384, 1024, 768, 512
Z_COLS = Z_MLA + Z_RET + Z_HY + Z_LRU


def _cparams(sem):
    return pltpu.CompilerParams(dimension_semantics=sem, vmem_limit_bytes=VMEM_LIMIT)


def _bdot(a, b):
    return jnp.dot(a.astype(BF16), b.astype(BF16), preferred_element_type=F32)


def _bdot_t(a, b):
    return lax.dot_general(a.astype(BF16), b.astype(BF16), (((1,), (1,)), ((), ())),
                           preferred_element_type=F32)


def _fdot(a, b):
    return jnp.dot(a, b, preferred_element_type=F32, precision=lax.Precision.HIGHEST)


def _sigmoid(x):
    return 0.5 * jnp.tanh(0.5 * x) + 0.5


def _shift_rows(x, d):
    n = x.shape[0]
    r = pltpu.roll(x, (n - d) % n, 0)
    row = lax.broadcasted_iota(jnp.int32, (n, 1), 0)
    ok = (row + d >= 0) & (row + d < n)
    return jnp.where(ok, r, 0.0)


def _mod_kernel(c_ref, w_ref, b_ref, o_ref):
    c = c_ref[...]
    a = c * jax.nn.sigmoid(c)
    o_ref[0] = _fdot(a, w_ref[0]) + b_ref[0]


def _modulation(cc, w_mod, b_mod):
    nl, d, n6 = w_mod.shape
    rows = cc.shape[0]
    tn = 1536
    return pl.pallas_call(
        _mod_kernel,
        out_shape=jax.ShapeDtypeStruct((nl, rows, n6), F32),
        grid=(nl, n6 // tn),
        in_specs=[pl.BlockSpec((rows, d), lambda l, j: (0, 0)),
                  pl.BlockSpec((1, d, tn), lambda l, j: (l, 0, j)),
                  pl.BlockSpec((1, 1, tn), lambda l, j: (l, 0, j))],
        out_specs=pl.BlockSpec((1, rows, tn), lambda l, j: (l, 0, j)),
        compiler_params=_cparams(("arbitrary", "arbitrary")),
        name="modulation",
    )(cc, w_mod, b_mod.reshape(nl, 1, n6))


def _stream_specs(src, nt_all, n_lat):
    if len(src) == 1:
        return [pl.BlockSpec((TM, D_MODEL), lambda b, t: (b * nt_all + t, 0))]
    return [pl.BlockSpec((1, TM, D_MODEL), lambda b, t: (b, jnp.minimum(t, n_lat - 1), 0)),
            pl.BlockSpec((1, TM, D_MODEL), lambda b, t: (b, 0, 0))]


def _stream_tile(refs, n_lat):
    if len(refs) == 1:
        return refs[0][...]
    return jnp.where(pl.program_id(1) < n_lat, refs[0][0], refs[1][0])


def _inproj_kernel(n_src, n_lat, *refs):
    x = _stream_tile(refs[:n_src], n_lat)
    mod_ref, g_ref, w_ref, zm_ref, zr_ref, zh_ref, zl_ref = refs[n_src:]
    ms = jnp.mean(x * x, axis=-1, keepdims=True)
    y = x * lax.rsqrt(ms + EPS) * g_ref[...]
    sh = mod_ref[0, :, 0:D_MODEL]
    sc = mod_ref[0, :, D_MODEL:2 * D_MODEL]
    a = y * (1.0 + sc) + sh
    z = _bdot(a, w_ref[...])
    o = 0
    for ref, w in ((zm_ref, Z_MLA), (zr_ref, Z_RET), (zh_ref, Z_HY), (zl_ref, Z_LRU)):
        ref[...] = z[:, o:o + w].astype(BF16)
        o += w


def _mod_index(nt_all, n_lat, n_batch):
    def idx(b, t):
        return (jnp.where(t < n_lat, b, n_batch), 0, 0)
    return idx


def _inproj(src, mod, g1, w_in_p, n_batch, nt, n_lat):
    rows = n_batch * nt * TM
    row_map = lambda b, t: (b * nt + t, 0)
    outs = [jax.ShapeDtypeStruct((rows, w), BF16) for w in (Z_MLA, Z_RET, Z_HY, Z_LRU)]
    return pl.pallas_call(
        functools.partial(_inproj_kernel, len(src), n_lat),
        out_shape=outs,
        grid=(n_batch, nt),
        in_specs=_stream_specs(src, nt, n_lat)
                 + [pl.BlockSpec((1, 1, 6 * D_MODEL), _mod_index(nt, n_lat, n_batch)),
                    pl.BlockSpec((1, D_MODEL), lambda b, t: (0, 0)),
                    pl.BlockSpec((D_MODEL, Z_COLS), lambda b, t: (0, 0))],
        out_specs=[pl.BlockSpec((TM, w), row_map) for w in (Z_MLA, Z_RET, Z_HY, Z_LRU)],
        compiler_params=_cparams(("parallel", "arbitrary")),
        name="inproj",
    )(*src, mod, g1, w_in_p)


def _mla_kernel(tq, q_row0, key_row0, n_key_tiles, z_ref, cos_ref, sin_ref, gq_ref, gkv_ref, gqm_ref,
                gqs_ref, gkm_ref, gkr_ref, gks_ref, wq_ref, wkv_ref, o_ref, k_scr, v_scr):
    t = pl.program_id(1)
    hw = MLA_HEADS * LANES
    lane = lax.broadcasted_iota(jnp.int32, (1, LANES), 1)
    is_rope = (lane >= MLA_NOPE) & (lane < MLA_QK)
    vlane = lax.broadcasted_iota(jnp.int32, (1, GROUP_W), 1)

    def kv_tile(i, carry):
        r0 = pl.multiple_of(i * TM, TM)
        rs0 = pl.multiple_of(key_row0 + i * TM, TM)
        z = z_ref[0, pl.ds(rs0, TM), LANES:3 * LANES].astype(F32)
        col1 = z[:, 0:LANES]
        zb = z[:, LANES:2 * LANES]
        cos = cos_ref[pl.ds(rs0, TM), :]
        sin = sin_ref[pl.ds(rs0, TM), :]
        ms_kv = jnp.mean(zb * zb, axis=-1, keepdims=True)
        kv = _bdot(zb * lax.rsqrt(ms_kv + EPS) * gkv_ref[...], wkv_ref[...])
        kr = jnp.where(is_rope, col1, 0.0)
        kr_sw = jnp.where(is_rope, pltpu.roll(col1, LANES - MLA_ROPE, 1), 0.0)
        k_rot = kr * gkr_ref[...] * cos + kr_sw * gks_ref[...] * sin
        ss_kr = jnp.sum(kr * kr, axis=-1, keepdims=True)
        for h in range(MLA_HEADS):
            kn = kv[:, h * LANES:(h + 1) * LANES]
            rk = lax.rsqrt((jnp.sum(kn * kn, axis=-1, keepdims=True) + ss_kr) / MLA_QK + EPS)
            k_scr[pl.ds(r0, TM), h * LANES:(h + 1) * LANES] = (
                rk * (kn * gkm_ref[...] + k_rot)).astype(BF16)
            vh = kv[:, hw + h * GROUP_W:hw + (h + 1) * GROUP_W]
            ones_lane = ((h + 1) % MLA_HEADS) * MLA_V
            v_scr[pl.ds(r0, TM), h * GROUP_W:(h + 1) * GROUP_W] = jnp.where(
                vlane == ones_lane, 1.0, vh).astype(BF16)
        return carry

    @pl.when(t == 0)
    def _():
        lax.fori_loop(0, n_key_tiles, kv_tile, 0)

    q0 = pl.multiple_of(q_row0 + t * tq, TM)
    za = z_ref[0, pl.ds(q0, tq), 0:2 * LANES].astype(F32)
    cos = cos_ref[pl.ds(q0, tq), :]
    sin = sin_ref[pl.ds(q0, tq), :]
    lane2 = lax.broadcasted_iota(jnp.int32, (1, 2 * LANES), 1)
    ms_q = jnp.sum(jnp.where(lane2 < MLA_Q_RANK, za * za, 0.0), axis=-1, keepdims=True) / MLA_Q_RANK
    qall = _bdot(za * lax.rsqrt(ms_q + EPS) * gq_ref[...], wq_ref[...])
    q_heads = []
    for h in range(MLA_HEADS):
        qh = qall[:, h * LANES:(h + 1) * LANES]
        qs = qall[:, hw + h * LANES:hw + (h + 1) * LANES]
        rs = lax.rsqrt(jnp.sum(qh * qh, axis=-1, keepdims=True) / MLA_QK + EPS) * (MLA_QK ** -0.5)
        q_heads.append((rs * (qh * gqm_ref[...] * cos + qs * gqs_ref[...] * sin)).astype(BF16))

    acc = jnp.zeros((tq, GROUP_W), F32)
    for h in range(MLA_HEADS):
        s = _bdot_t(q_heads[h], k_scr[:, h * LANES:(h + 1) * LANES])
        m = jnp.max(s, axis=-1, keepdims=True)
        p = jnp.exp((s - m).astype(BF16))
        pv = jnp.dot(p, v_scr[:, h * GROUP_W:(h + 1) * GROUP_W], preferred_element_type=F32)
        ones_lane = ((h + 1) % MLA_HEADS) * MLA_V
        l = jnp.sum(jnp.where(vlane == ones_lane, pv, 0.0), axis=-1, keepdims=True)
        in_head = (vlane >= h * MLA_V) & (vlane < (h + 1) * MLA_V)
        acc = acc + jnp.where(in_head, pv * (1.0 / l), 0.0)
    o_ref[0] = acc.astype(BF16)


def _mla(zm, cos_t, sin_t, gains, wq, wkv, n_batch, tq, q_row0, n_q_tiles, key_row0, n_key_tiles):
    seq = zm.shape[1]
    hw = MLA_HEADS * LANES
    vw = MLA_HEADS * GROUP_W
    n_keys = n_key_tiles * TM
    small = lambda w: pl.BlockSpec((1, w), lambda b, t: (0, 0))
    return pl.pallas_call(
        functools.partial(_mla_kernel, tq, q_row0, key_row0, n_key_tiles),
        out_shape=jax.ShapeDtypeStruct((n_batch, n_q_tiles * tq, GROUP_W), BF16),
        grid=(n_batch, n_q_tiles),
        in_specs=[pl.BlockSpec((1, seq, Z_MLA), lambda b, t: (b, 0, 0)),
                  pl.BlockSpec((seq, LANES), lambda b, t: (0, 0)),
                  pl.BlockSpec((seq, LANES), lambda b, t: (0, 0)),
                  small(2 * LANES), small(LANES), small(LANES), small(LANES), small(LANES),
                  small(LANES), small(LANES),
                  pl.BlockSpec((2 * LANES, 2 * hw), lambda b, t: (0, 0)),
                  pl.BlockSpec((LANES, hw + vw), lambda b, t: (0, 0))],
        out_specs=pl.BlockSpec((1, tq, GROUP_W), lambda b, t: (b, t, 0)),
        scratch_shapes=[pltpu.VMEM((n_keys, hw), BF16), pltpu.VMEM((n_keys, vw), BF16)],
        compiler_params=_cparams(("parallel", "arbitrary")),
        name="mla_attention",
    )(zm, cos_t, sin_t, *gains, wq, wkv)


def _ret_kernel(n_lat, ctx_out, z_ref, dm_ref, dq_ref, dk_ref, gc_ref, bd_ref, g_ref,
                o_ref, acc_ref, st_ref):
    lane = lax.broadcasted_iota(jnp.int32, (1, GROUP_W), 1)
    hmask = [(lane >= h * RET_DK) & (lane < (h + 1) * RET_DK) for h in range(RET_HEADS)]
    ones_bd = bd_ref[...].astype(BF16)

    def qkv(r0):
        q = z_ref[0, pl.ds(r0, TM), 0:GROUP_W]
        k = z_ref[0, pl.ds(r0, TM), GROUP_W:2 * GROUP_W].astype(F32) * (RET_DK ** -0.5)
        v = z_ref[0, pl.ds(r0, TM), 2 * GROUP_W:3 * GROUP_W]
        return q, k, v

    def inner(q, k, v):
        kb = k.astype(BF16)
        acc = jnp.zeros((TM, GROUP_W), F32)
        for h in range(RET_HEADS):
            qh = jnp.where(hmask[h], q, jnp.zeros_like(q))
            vh = jnp.where(hmask[h], v, jnp.zeros_like(v))
            s = _bdot_t(qh, kb) * dm_ref[h]
            acc = acc + _bdot(s, vh)
        return acc

    def state_update(d, k, v):
        kd = (k * dk_ref[d]).T
        st_ref[...] = st_ref[...] * gc_ref[d] + _bdot(kd, v) * bd_ref[...]

    def cross(d, q):
        return _bdot(q.astype(F32) * dq_ref[d], st_ref[...])

    def finish(r0, o):
        sq = o * o
        hi = sq.astype(BF16)
        lo = (sq - hi.astype(F32)).astype(BF16)
        ms = (jnp.dot(hi, ones_bd, preferred_element_type=F32)
              + jnp.dot(lo, ones_bd, preferred_element_type=F32)) / RET_DK
        gate = z_ref[0, pl.ds(r0, TM), 3 * GROUP_W:4 * GROUP_W].astype(F32)
        y = o * lax.rsqrt(ms + EPS) * g_ref[...] * (gate * _sigmoid(gate))
        o_ref[0, pl.ds(r0, TM), :] = y.astype(BF16)

    c0 = n_lat * TM
    qc, kc, vc = qkv(c0)
    if ctx_out:
        finish(c0, inner(qc, kc, vc))
    else:
        o_ref[0, pl.ds(c0, TM), :] = jnp.zeros((TM, GROUP_W), BF16)

    st_ref[...] = jnp.zeros_like(st_ref)
    state_update(0, kc, vc)

    def fwd(n, carry):
        r0 = pl.multiple_of(n * TM, TM)
        q, k, v = qkv(r0)
        acc_ref[pl.ds(r0, TM), :] = inner(q, k, v) + cross(0, q)
        state_update(0, k, v)
        return carry

    lax.fori_loop(0, n_lat, fwd, 0, unroll=RET_UNROLL)

    st_ref[...] = jnp.zeros_like(st_ref)
    state_update(1, kc, vc)

    def bwd(i, carry):
        n = n_lat - 1 - i
        r0 = pl.multiple_of(n * TM, TM)
        q, k, v = qkv(r0)
        finish(r0, acc_ref[pl.ds(r0, TM), :] + cross(1, q))
        state_update(1, k, v)
        return carry

    lax.fori_loop(0, n_lat, bwd, 0, unroll=RET_UNROLL)


def _retention(zr, tabs, g, n_batch, n_lat, ctx_out):
    seq = zr.shape[1]
    dm, dq, dk, gc, bd = tabs
    full = lambda a: pl.BlockSpec(a.shape, lambda b: (0,) * a.ndim)
    return pl.pallas_call(
        functools.partial(_ret_kernel, n_lat, ctx_out),
        out_shape=jax.ShapeDtypeStruct((n_batch, seq, GROUP_W), BF16),
        grid=(n_batch,),
        in_specs=[pl.BlockSpec((1, seq, Z_RET), lambda b: (b, 0, 0)),
                  full(dm), full(dq), full(dk), full(gc), full(bd), full(g)],
        out_specs=pl.BlockSpec((1, seq, GROUP_W), lambda b: (b, 0, 0)),
        scratch_shapes=[pltpu.VMEM((n_lat * TM, GROUP_W), F32),
                        pltpu.VMEM((GROUP_W, GROUP_W), F32)],
        compiler_params=_cparams(("parallel",)),
        name="retention",
    )(zr, dm, dq, dk, gc, bd, g)


def _retention_tables(lg):
    j = jnp.arange(TM, dtype=F32)
    rel = j[:, None] - j[None, :]
    lf = lg[0][:, None, None]
    lb = lg[1][:, None, None]
    dm = (jnp.where(rel[None] >= 0, jnp.exp(jnp.maximum(rel, 0.0)[None] * lf), 0.0)
          + jnp.where(rel[None] <= 0, jnp.exp(jnp.maximum(-rel, 0.0)[None] * lb), 0.0))
    lane_f = jnp.repeat(lg[0], RET_DK)[None, :]
    lane_b = jnp.repeat(lg[1], RET_DK)[None, :]
    dq = jnp.stack([jnp.exp((j + 1)[:, None] * lane_f), jnp.exp((TM - j)[:, None] * lane_b)])
    dk = jnp.stack([jnp.exp((TM - 1 - j)[:, None] * lane_f), jnp.exp(j[:, None] * lane_b)])
    gc = jnp.stack([jnp.exp(TM * lane_f).reshape(GROUP_W, 1), jnp.exp(TM * lane_b).reshape(GROUP_W, 1)])
    hid = jnp.arange(GROUP_W) // RET_DK
    bd = (hid[:, None] == hid[None, :]).astype(F32)
    return dm, dq, dk, gc, bd


def _hy_pre_kernel(n_lat, z_ref, w_ref, b_ref, s_ref, x0_ref):
    def seg(r0, n):
        z = z_ref[0, r0:r0 + n, :].astype(F32)
        u = (_shift_rows(z, -1) * w_ref[0:1, :] + z * w_ref[1:2, :]
             + _shift_rows(z, 1) * w_ref[2:3, :] + b_ref[...])
        x0_ref[0, r0:r0 + n, :] = u[:, 0:GROUP_W].astype(BF16)
        s_ref[0, r0:r0 + n, :] = (u[:, GROUP_W:2 * GROUP_W] * u[:, 2 * GROUP_W:]).astype(BF16)

    seg(0, n_lat * TM)
    seg(n_lat * TM, z_ref.shape[1] - n_lat * TM)


def _hy_pre(zh, w, b, n_batch, n_lat):
    seq = zh.shape[1]
    return pl.pallas_call(
        functools.partial(_hy_pre_kernel, n_lat),
        out_shape=[jax.ShapeDtypeStruct((n_batch, seq, GROUP_W), BF16)] * 2,
        grid=(n_batch,),
        in_specs=[pl.BlockSpec((1, seq, Z_HY), lambda i: (i, 0, 0)),
                  pl.BlockSpec(w.shape, lambda i: (0, 0)),
                  pl.BlockSpec(b.shape, lambda i: (0, 0))],
        out_specs=[pl.BlockSpec((1, seq, GROUP_W), lambda i: (i, 0, 0))] * 2,
        compiler_params=_cparams(("parallel",)),
        name="hyena_pre",
    )(zh, w, b)


def _hy_filter_kernel(z_ref, win_ref, w1_ref, b1_ref, w2_ref, b2_ref, w3_ref, fr_ref, hf_ref, hb_ref):
    fr = fr_ref[...]
    h = jnp.sin(fr * (_fdot(z_ref[...], w1_ref[...]) + b1_ref[...]))
    h = jnp.sin(fr * (_fdot(h, w2_ref[...]) + b2_ref[...]))
    filt = _fdot(h, w3_ref[...])
    win = win_ref[...]
    hf_ref[...] = filt[:, 0:GROUP_W] * win
    hb_ref[...] = filt[:, GROUP_W:] * win


def _hy_filter(n, w1p, b1, w2, b2, w3, freq):
    t = jnp.linspace(0.0, 1.0, n, dtype=F32)[:, None]
    bands = jnp.linspace(1e-4, HY_BANDS - 1, HY_BANDS, dtype=F32)
    w = 2.0 * math.pi * jnp.arange(n, dtype=F32)[:, None] / n
    z = jnp.concatenate([t, jnp.cos(bands * w), -jnp.sin(bands * w)], axis=-1)
    z = jnp.pad(z, ((0, 0), (0, LANES - z.shape[1])))
    max_decay = math.log(HY_TARGET) / HY_FAST_PCT
    min_decay = math.log(HY_TARGET) / HY_SLOW_PCT
    deltas = jnp.abs(jnp.linspace(min_decay, max_decay, GROUP_W, dtype=F32))
    window = jnp.exp(-t * deltas)
    full = lambda a: pl.BlockSpec(a.shape, lambda i: (0, 0))
    rows = lambda wd: pl.BlockSpec((TM, wd), lambda i: (i, 0))
    return pl.pallas_call(
        _hy_filter_kernel,
        out_shape=[jax.ShapeDtypeStruct((n, GROUP_W), F32)] * 2,
        grid=(n // TM,),
        in_specs=[rows(LANES), rows(GROUP_W), full(w1p), full(b1), full(w2), full(b2), full(w3),
                  full(freq)],
        out_specs=[rows(GROUP_W)] * 2,
        compiler_params=_cparams(("arbitrary",)),
        name="hyena_filter",
    )(z, window, w1p, b1, w2, b2, w3, freq)


def _toeplitz_rows(hf, hb, nj):
    n = hf.shape[0]
    full = jnp.concatenate([hf, jnp.zeros((1, GROUP_W), F32), hb[1:][::-1]], axis=0)
    d = jnp.arange(-(nj - 1), nj)[:, None]
    m = jnp.arange(2 * TM)[None, :]
    lag = d * TM + jnp.where(m < TM, m, m - 2 * TM)
    idx = jnp.mod(lag, 2 * n)
    return jnp.transpose(full[idx], (2, 0, 1))


def _hy_conv_kernel(nj, nb, s_ref, w_ref, y_ref, acc_ref):
    def chan(c, carry):
        sc = s_ref[c].reshape(nj * nb, TM)
        acc_ref[...] = jnp.zeros_like(acc_ref)
        for di in range(2 * nj - 1):
            d = di - (nj - 1)
            w = w_ref[c, di:di + 1, :]
            rolled = pltpu.roll(jnp.broadcast_to(w, (TM, 2 * TM)), 0, 1, stride=1, stride_axis=0)
            toep = rolled[:, 0:TM].astype(BF16)
            if d >= 0:
                res = jnp.dot(sc[0:(nj - d) * nb], toep, preferred_element_type=F32)
                acc_ref[d * nb:nj * nb, :] += res
            else:
                res = jnp.dot(sc[(-d) * nb:nj * nb], toep, preferred_element_type=F32)
                acc_ref[0:(nj + d) * nb, :] += res
        y_ref[c] = acc_ref[...].reshape(nj, nb, TM).astype(BF16)
        return carry

    lax.fori_loop(0, HY_CB, chan, 0)


def _hy_conv(s_t, w_rows):
    ch, nj, nb, _ = s_t.shape
    nd = 2 * nj - 1
    return pl.pallas_call(
        functools.partial(_hy_conv_kernel, nj, nb),
        out_shape=jax.ShapeDtypeStruct(s_t.shape, BF16),
        grid=(ch // HY_CB,),
        in_specs=[pl.BlockSpec((HY_CB, nj, nb, TM), lambda i: (i, 0, 0, 0)),
                  pl.BlockSpec((HY_CB, nd, 2 * TM), lambda i: (i, 0, 0))],
        out_specs=pl.BlockSpec((HY_CB, nj, nb, TM), lambda i: (i, 0, 0, 0)),
        scratch_shapes=[pltpu.VMEM((nj * nb, TM), F32)],
        compiler_params=_cparams(("parallel",)),
        name="hyena_conv",
    )(s_t, w_rows)


def _lru_tile_scan(a, b, carry, reverse):
    row = lax.broadcasted_iota(jnp.int32, (SUBLANES, 1), 0)
    for k in (1, 2, 4):
        if reverse:
            ok = row < SUBLANES - k
            sh = SUBLANES - k
        else:
            ok = row >= k
            sh = k
        a_s = jnp.where(ok, pltpu.roll(a, sh, 0), 1.0)
        b_s = jnp.where(ok, pltpu.roll(b, sh, 0), 0.0)
        b = a * b_s + b
        a = a * a_s
    return b + a * carry


def _lru_kernel(n_lat, z_ref, cw_ref, cb_ref, wg_ref, bg_ref, sp_ref, o_ref,
                af_ref, bf_ref, ab_ref, bb_ref, h_ref):
    seq = z_ref.shape[1]
    s_lat = n_lat * TM

    def prep(r0, n):
        x = z_ref[0, r0:r0 + n, 0:GROUP_W].astype(F32)
        x = (_shift_rows(x, -2) * cw_ref[0:1, :] + _shift_rows(x, -1) * cw_ref[1:2, :]
             + x * cw_ref[2:3, :] + _shift_rows(x, 1) * cw_ref[3:4, :] + cb_ref[...])
        g = _bdot(x, wg_ref[...]) + bg_ref[...]
        for d, (a_ref, b_ref) in enumerate(((af_ref, bf_ref), (ab_ref, bb_ref))):
            r = _sigmoid(g[:, (2 * d) * GROUP_W:(2 * d + 1) * GROUP_W])
            i = _sigmoid(g[:, (2 * d + 1) * GROUP_W:(2 * d + 2) * GROUP_W])
            a = jnp.exp(-LRU_C * r * sp_ref[d:d + 1, :])
            a_ref[r0:r0 + n, :] = a
            b_ref[r0:r0 + n, :] = jnp.sqrt(1.0 - a * a) * (i * x)

    prep(0, s_lat)
    prep(s_lat, seq - s_lat)

    n_ctx_tiles = (seq - s_lat) // SUBLANES
    n_lat_tiles = s_lat // SUBLANES

    def fwd_tile(base):
        def body(i, carry):
            r0 = pl.multiple_of(base + i * SUBLANES, SUBLANES)
            h = _lru_tile_scan(af_ref[pl.ds(r0, SUBLANES), :], bf_ref[pl.ds(r0, SUBLANES), :],
                               carry, False)
            h_ref[pl.ds(r0, SUBLANES), :] = h
            return h[SUBLANES - 1:SUBLANES, :]
        return body

    carry = jnp.zeros((1, GROUP_W), F32)
    carry = lax.fori_loop(0, n_ctx_tiles, fwd_tile(s_lat), carry, unroll=SCAN_UNROLL)
    lax.fori_loop(0, n_lat_tiles, fwd_tile(0), carry, unroll=SCAN_UNROLL)

    def bwd_tile(base, n_tiles):
        def body(i, carry):
            r0 = pl.multiple_of(base + (n_tiles - 1 - i) * SUBLANES, SUBLANES)
            h = _lru_tile_scan(ab_ref[pl.ds(r0, SUBLANES), :], bb_ref[pl.ds(r0, SUBLANES), :],
                               carry, True)
            gate = z_ref[0, pl.ds(r0, SUBLANES), GROUP_W:2 * GROUP_W].astype(F32)
            y = (h + h_ref[pl.ds(r0, SUBLANES), :]) * jax.nn.gelu(gate, approximate=True)
            o_ref[0, pl.ds(r0, SUBLANES), :] = y.astype(BF16)
            return h[0:1, :]
        return body

    carry = jnp.zeros((1, GROUP_W), F32)
    carry = lax.fori_loop(0, n_ctx_tiles, bwd_tile(s_lat, n_ctx_tiles), carry, unroll=SCAN_UNROLL)
    lax.fori_loop(0, n_lat_tiles, bwd_tile(0, n_lat_tiles), carry, unroll=SCAN_UNROLL)


def _lru(zl, cw, cb, wg, bg, sp, n_batch, n_lat):
    seq = zl.shape[1]
    full = lambda a: pl.BlockSpec(a.shape, lambda i: (0,) * a.ndim)
    return pl.pallas_call(
        functools.partial(_lru_kernel, n_lat),
        out_shape=jax.ShapeDtypeStruct((n_batch, seq, GROUP_W), BF16),
        grid=(n_batch,),
        in_specs=[pl.BlockSpec((1, seq, Z_LRU), lambda i: (i, 0, 0)),
                  full(cw), full(cb), full(wg), full(bg), full(sp)],
        out_specs=pl.BlockSpec((1, seq, GROUP_W), lambda i: (i, 0, 0)),
        scratch_shapes=[pltpu.VMEM((seq, GROUP_W), F32)] * 5,
        compiler_params=_cparams(("parallel",)),
        name="rglru",
    )(zl, cw, cb, wg, bg, sp)


def _merge_kernel(n_src, n_lat, *refs):
    x_in = _stream_tile(refs[:n_src], n_lat)
    (ya_ref, yb_ref, yc_ref, s_ref, x0_ref, yd_ref, hd_ref, gg_ref, wo_ref,
     mod_ref, g2_ref, wr_ref, x1_ref, f_ref, rt_ref, cat_ref) = refs[n_src:]
    s = s_ref[0].astype(F32)
    yh = x0_ref[0].astype(F32) * (yc_ref[0].astype(F32) + s * hd_ref[...])
    parts = (ya_ref[0].astype(F32), yb_ref[0].astype(F32), yh, yd_ref[0].astype(F32))
    for g, y in enumerate(parts):
        ms = jnp.mean(y * y, axis=-1, keepdims=True)
        gg = gg_ref[:, g * GROUP_W:(g + 1) * GROUP_W]
        cat_ref[:, g * GROUP_W:(g + 1) * GROUP_W] = (y * lax.rsqrt(ms + EPS) * gg).astype(BF16)
    m = jnp.dot(cat_ref[...], wo_ref[...], preferred_element_type=F32)
    gate1 = mod_ref[0, :, 2 * D_MODEL:3 * D_MODEL]
    x1 = x_in + gate1 * m
    x1_ref[...] = x1
    ms = jnp.mean(x1 * x1, axis=-1, keepdims=True)
    sh = mod_ref[0, :, 3 * D_MODEL:4 * D_MODEL]
    sc = mod_ref[0, :, 4 * D_MODEL:5 * D_MODEL]
    f = (x1 * lax.rsqrt(ms + EPS) * g2_ref[...]) * (1.0 + sc) + sh
    fb = f.astype(BF16)
    f_ref[...] = fb
    rt_ref[...] = _route(jnp.dot(fb, wr_ref[...], preferred_element_type=F32))


def _route(logits):
    lane = lax.broadcasted_iota(jnp.int32, logits.shape, 1)
    neg = jnp.float32(-jnp.inf)
    big = jnp.int32(LANES)
    gl = jnp.where(lane < MOE_GROUPS, logits, neg)
    gmax = jnp.max(gl, axis=-1, keepdims=True)
    gsum = jnp.sum(jnp.exp(gl - gmax), axis=-1, keepdims=True)
    g_val = 1.0 / gsum
    g_idx = jnp.min(jnp.where(gl == gmax, lane, big), axis=-1, keepdims=True)
    lo = MOE_GROUPS + MOE_PER_GROUP * g_idx
    el = jnp.where((lane >= lo) & (lane < lo + MOE_PER_GROUP), logits, neg)
    m1 = jnp.max(el, axis=-1, keepdims=True)
    i1 = jnp.min(jnp.where(el == m1, lane, big), axis=-1, keepdims=True)
    el2 = jnp.where(lane == i1, neg, el)
    m2 = jnp.max(el2, axis=-1, keepdims=True)
    i2 = jnp.min(jnp.where(el2 == m2, lane, big), axis=-1, keepdims=True)
    r = jnp.exp(m2 - m1)
    w1 = g_val / (1.0 + r)
    w2 = w1 * r
    e1 = (i1 - MOE_GROUPS).astype(F32)
    e2 = (i2 - MOE_GROUPS).astype(F32)
    return jnp.where(lane == 0, e1, jnp.where(lane == 1, e2, jnp.where(lane == 2, w1,
                     jnp.where(lane == 3, w2, 0.0))))


def _merge(src, ys, hd, gg, wo, mod, g2, wr, n_batch, nt_all, nt, n_lat):
    rows_all = n_batch * nt_all * TM
    rows_out = n_batch * nt * TM
    in_rows = lambda b, t: (b * nt_all + t, 0)
    out_rows = lambda b, t: (b * nt + t, 0)
    seq_blk = pl.BlockSpec((1, TM, GROUP_W), lambda b, t: (b, t, 0))
    small = lambda a: pl.BlockSpec(a.shape, lambda b, t: (0, 0))
    return pl.pallas_call(
        functools.partial(_merge_kernel, len(src), n_lat),
        out_shape=[jax.ShapeDtypeStruct((rows_all, D_MODEL), F32),
                   jax.ShapeDtypeStruct((rows_out, D_MODEL), BF16),
                   jax.ShapeDtypeStruct((rows_out, LANES), F32)],
        grid=(n_batch, nt),
        in_specs=_stream_specs(src, nt_all, n_lat) + [seq_blk] * 6
                 + [small(hd), small(gg), small(wo),
                    pl.BlockSpec((1, 1, 6 * D_MODEL), _mod_index(nt_all, n_lat, n_batch)),
                    small(g2), small(wr)],
        out_specs=[pl.BlockSpec((TM, D_MODEL), in_rows),
                   pl.BlockSpec((TM, D_MODEL), out_rows),
                   pl.BlockSpec((TM, LANES), out_rows)],
        scratch_shapes=[pltpu.VMEM((TM, D_MODEL), BF16)],
        input_output_aliases={0: 0} if len(src) == 1 else {},
        compiler_params=_cparams(("parallel", "arbitrary")),
        name="merge_outproj",
    )(*src, *ys, hd, gg, wo, mod, g2, wr)


def _expert_kernel(be_ref, nv_ref, x_ref, w1_ref, w3_ref, w2_ref, o_ref, w1b_ref, w3b_ref, w2b_ref):
    i = pl.program_id(0)

    @pl.when((i == 0) | (be_ref[i] != be_ref[jnp.maximum(i - 1, 0)]))
    def _():
        w1b_ref[...] = w1_ref[0, 0].astype(BF16)
        w3b_ref[...] = w3_ref[0, 0].astype(BF16)
        w2b_ref[...] = w2_ref[0, 0].astype(BF16)

    @pl.when(i < nv_ref[0])
    def _():
        x = x_ref[...]
        a = jnp.dot(x, w1b_ref[...], preferred_element_type=F32)
        b = jnp.dot(x, w3b_ref[...], preferred_element_type=F32)
        h = (a * _sigmoid(a)) * b
        o_ref[...] = jnp.dot(h.astype(BF16), w2b_ref[...], preferred_element_type=F32).astype(BF16)

    @pl.when(i >= nv_ref[0])
    def _():
        o_ref[...] = jnp.zeros_like(o_ref)


def _experts(xs, block_exp, n_valid, w1, w3, w2, layer):
    n_slots = xs.shape[0]
    n_blocks = n_slots // MOE_BM
    w_idx = lambda i, be, nv: (layer, be[i], 0, 0)
    grid_spec = pltpu.PrefetchScalarGridSpec(
        num_scalar_prefetch=2,
        grid=(n_blocks,),
        in_specs=[pl.BlockSpec((MOE_BM, D_MODEL), lambda i, be, nv: (i, 0)),
                  pl.BlockSpec((1, 1, D_MODEL, EXPERT_FF), w_idx),
                  pl.BlockSpec((1, 1, D_MODEL, EXPERT_FF), w_idx),
                  pl.BlockSpec((1, 1, EXPERT_FF, D_MODEL), w_idx)],
        out_specs=pl.BlockSpec((MOE_BM, D_MODEL), lambda i, be, nv: (i, 0)),
        scratch_shapes=[pltpu.VMEM((D_MODEL, EXPERT_FF), BF16),
                        pltpu.VMEM((D_MODEL, EXPERT_FF), BF16),
                        pltpu.VMEM((EXPERT_FF, D_MODEL), BF16)],
    )
    return pl.pallas_call(
        _expert_kernel,
        out_shape=jax.ShapeDtypeStruct((n_slots, D_MODEL), BF16),
        grid_spec=grid_spec,
        compiler_params=_cparams(("arbitrary",)),
        name="moe_experts",
    )(block_exp, n_valid, xs, w1, w3, w2)


def _combine_kernel(x_ref, ya_ref, yb_ref, rt_ref, mod_ref, o_ref):
    gate2 = mod_ref[0, :, 5 * D_MODEL:6 * D_MODEL]
    ya = ya_ref[...].astype(F32)
    yb = yb_ref[...].astype(F32)
    o_ref[...] = x_ref[...] + gate2 * (rt_ref[:, 2:3] * ya + rt_ref[:, 3:4] * yb)


def _combine(xx, ya, yb, route, mod, n_batch, nt_all, nt, n_lat, in_place):
    in_rows = lambda b, t: (b * nt_all + t, 0)
    out_rows = lambda b, t: (b * nt + t, 0)
    rows = xx.shape[0] if in_place else n_batch * nt * TM
    return pl.pallas_call(
        _combine_kernel,
        out_shape=jax.ShapeDtypeStruct((rows, D_MODEL), F32),
        grid=(n_batch, nt),
        in_specs=[pl.BlockSpec((TM, D_MODEL), in_rows),
                  pl.BlockSpec((TM, D_MODEL), out_rows),
                  pl.BlockSpec((TM, D_MODEL), out_rows),
                  pl.BlockSpec((TM, LANES), out_rows),
                  pl.BlockSpec((1, 1, 6 * D_MODEL), _mod_index(nt_all, n_lat, n_batch))],
        out_specs=pl.BlockSpec((TM, D_MODEL), in_rows if in_place else out_rows),
        input_output_aliases={0: 0} if in_place else {},
        compiler_params=_cparams(("parallel", "arbitrary")),
        name="moe_combine",
    )(xx, ya, yb, route, mod)


def _dispatch_plan(eid):
    t = eid.shape[0]
    n_assign = t * TOP_K
    key_bits = (n_assign - 1).bit_length()
    assert key_bits + (N_EXPERTS - 1).bit_length() <= 31
    e_flat = eid.reshape(-1)
    experts = jnp.arange(N_EXPERTS, dtype=jnp.int32)
    keys = lax.sort((e_flat << key_bits) | jnp.arange(n_assign, dtype=jnp.int32), is_stable=False)
    a_sorted = keys & ((1 << key_bits) - 1)
    e_sorted = keys >> key_bits
    counts = jnp.sum((e_flat[:, None] == experts[None, :]).astype(jnp.int32), axis=0)
    padded = (counts + MOE_BM - 1) // MOE_BM * MOE_BM
    pad_end = jnp.cumsum(padded)
    shift = (pad_end - padded) - (jnp.cumsum(counts) - counts)
    n_blocks = -(-n_assign // MOE_BM) + N_EXPERTS
    n_slots = n_blocks * MOE_BM
    blk0 = jnp.arange(n_blocks, dtype=jnp.int32) * MOE_BM
    block_exp = jnp.minimum(jnp.sum((pad_end[None, :] <= blk0[:, None]).astype(jnp.int32), axis=1),
                            N_EXPERTS - 1)
    n_valid = (pad_end[-1] // MOE_BM).astype(jnp.int32).reshape(1)
    block_shift = jnp.sum(jnp.where(block_exp[:, None] == experts[None, :], shift[None, :], 0), axis=1)
    src = jnp.arange(n_slots, dtype=jnp.int32) - jnp.repeat(block_shift, MOE_BM)
    slot_tok = a_sorted.at[jnp.clip(src, 0, n_assign - 1)].get(mode='promise_in_bounds') // TOP_K
    dest = (jnp.arange(n_assign, dtype=jnp.int32)
            + jnp.sum(jnp.where(e_sorted[:, None] == experts[None, :], shift[None, :], 0), axis=1))
    _, pos = lax.sort((a_sorted, dest), num_keys=1)
    return slot_tok, pos.reshape(t, TOP_K), block_exp, n_valid


def _take_rows(a, idx):
    return a.at[idx].get(mode='promise_in_bounds')


def _moe(f, route, w1, w3, w2, layer):
    eid = route[:, 0:TOP_K].astype(jnp.int32)
    slot_tok, pos, block_exp, n_valid = _dispatch_plan(eid)
    ys = _experts(_take_rows(f, slot_tok), block_exp, n_valid, w1, w3, w2, layer)
    return _take_rows(ys, pos[:, 0]), _take_rows(ys, pos[:, 1])


def _pack_w_in(w_in):
    o = 0
    cols = {}
    for name, wd in (('cq', 192), ('ckv', 128), ('kr', 32), ('ret', 1024), ('hy', 768), ('lru', 512)):
        cols[name] = w_in[:, o:o + wd]
        o += wd
    swap = jnp.arange(MLA_ROPE) ^ 8
    kr_sw = cols['kr'][:, swap]
    return jnp.concatenate([cols['cq'], cols['kr'], kr_sw, cols['ckv'], cols['ret'], cols['hy'],
                            cols['lru']], axis=1).astype(BF16)


def _pack_mla(w_uq, w_ukv, q_norm_g, kv_norm_g, qn_g, kn_g):
    hw = MLA_HEADS * LANES
    swap = jnp.arange(MLA_ROPE) ^ 8
    eye = jnp.eye(MLA_HEADS, dtype=F32)
    q3 = w_uq.reshape(MLA_Q_RANK, MLA_HEADS, MLA_QK)
    q_main = jnp.pad(q3, ((0, 0), (0, 0), (0, LANES - MLA_QK)))
    q_swap = jnp.pad(q3[:, :, MLA_NOPE + swap], ((0, 0), (0, 0), (MLA_NOPE, LANES - MLA_QK)))
    wq = jnp.concatenate([q_main.reshape(MLA_Q_RANK, hw), q_swap.reshape(MLA_Q_RANK, hw)], axis=1)
    wq = jnp.pad(wq, ((0, 2 * LANES - MLA_Q_RANK), (0, 0)))
    kv3 = w_ukv.reshape(MLA_KV_RANK, MLA_HEADS, MLA_NOPE + MLA_V)
    k_part = jnp.pad(kv3[:, :, :MLA_NOPE], ((0, 0), (0, 0), (0, LANES - MLA_NOPE)))
    v_part = kv3[:, :, None, MLA_NOPE:] * eye[None, :, :, None]
    wkv = jnp.concatenate([k_part.reshape(MLA_KV_RANK, hw),
                           v_part.reshape(MLA_KV_RANK, MLA_HEADS * GROUP_W)], axis=1)
    pad = lambda v, lo, n: jnp.pad(v, (lo, n - lo - v.shape[0])).reshape(1, n)
    gains = (pad(q_norm_g, 0, 2 * LANES), kv_norm_g.reshape(1, LANES),
             pad(qn_g, 0, LANES), pad(qn_g[MLA_NOPE + swap], MLA_NOPE, LANES),
             pad(kn_g[:MLA_NOPE], 0, LANES), pad(kn_g[MLA_NOPE:], MLA_NOPE, LANES),
             pad(kn_g[MLA_NOPE + swap], MLA_NOPE, LANES))
    return wq.astype(BF16), wkv.astype(BF16), gains


def _rope_tables(s_lat, seq):
    rows = s_lat // GRID_W
    row = jnp.repeat(jnp.arange(rows), GRID_W).astype(F32)
    col = jnp.tile(jnp.arange(GRID_W), rows).astype(F32)
    half = MLA_ROPE // 4
    inv_freq = ROPE_BASE ** (-jnp.arange(half, dtype=F32) / half)
    ar = row[:, None] * inv_freq
    ac = col[:, None] * inv_freq
    cos32 = jnp.concatenate([jnp.cos(ar), jnp.cos(ar), jnp.cos(ac), jnp.cos(ac)], axis=1)
    sin32 = jnp.concatenate([-jnp.sin(ar), jnp.sin(ar), -jnp.sin(ac), jnp.sin(ac)], axis=1)
    lanes = ((MLA_NOPE, LANES - MLA_QK),)
    cos_t = jnp.pad(jnp.pad(cos32, ((0, 0),) + lanes, constant_values=1.0),
                    ((0, seq - s_lat), (0, 0)), constant_values=1.0)
    sin_t = jnp.pad(sin32, ((0, seq - s_lat),) + lanes)
    return cos_t, sin_t


def _block_diag(w):
    nb, bw, _ = w.shape
    eye = jnp.eye(nb, dtype=w.dtype)
    return (w[:, :, None, :] * eye[:, None, :, None]).reshape(nb * bw, nb * bw)


def kernel(x, c, ctx, c_ctx, w_mod, b_mod, norm1_g, norm2_g, w_in, mla_q_norm_g, mla_w_uq, mla_kv_norm_g, mla_w_ukv, mla_qn_g, mla_kn_g, ret_log_gamma, ret_norm_g, hy_conv_w, hy_conv_b, hy_w1, hy_b1, hy_w2, hy_b2, hy_w3, hy_freq, hy_d, lru_conv_w, lru_conv_b, lru_wa, lru_ba, lru_wx, lru_bx, lru_lambda, group_norm_g, w_out, moe_w_group, moe_w_expert, moe_w1, moe_w3, moe_w2):
    n_batch, s_lat, d = x.shape
    n_ctx = ctx.shape[1]
    depth = w_mod.shape[0]
    assert d == D_MODEL and n_ctx == TM and s_lat % TM == 0 and s_lat % GRID_W == 0
    n_lat = s_lat // TM
    nt_all = n_lat + 1
    seq = nt_all * TM
    mod_rows = -(-(n_batch + 1) // SUBLANES) * SUBLANES

    cc = jnp.concatenate([c, c_ctx[None, :], jnp.zeros((mod_rows - n_batch - 1, d), F32)], axis=0)
    mod_all = _modulation(cc, w_mod, b_mod)
    src = (x, ctx)
    cos_t, sin_t = _rope_tables(s_lat, seq)

    for l in range(depth):
        ctx_out = l < depth - 1
        nt = nt_all if ctx_out else n_lat
        mod = mod_all[l].reshape(mod_rows, 1, 6 * d)

        zm, zr, zh, zl = _inproj(src, mod, norm1_g[l].reshape(1, d), _pack_w_in(w_in[l]),
                                 n_batch, nt_all, n_lat)
        zm = zm.reshape(n_batch, seq, Z_MLA)
        zr = zr.reshape(n_batch, seq, Z_RET)
        zh = zh.reshape(n_batch, seq, Z_HY)
        zl = zl.reshape(n_batch, seq, Z_LRU)

        wq, wkv, gains = _pack_mla(mla_w_uq[l], mla_w_ukv[l], mla_q_norm_g[l], mla_kv_norm_g[l],
                                   mla_qn_g[l], mla_kn_g[l])
        tq = MLA_TQ if s_lat % MLA_TQ == 0 else TM
        y_mla = _mla(zm, cos_t, sin_t, gains, wq, wkv, n_batch, tq, 0, s_lat // tq, 0, nt_all)
        if ctx_out:
            y_mla = jnp.concatenate(
                [y_mla, _mla(zm, cos_t, sin_t, gains, wq, wkv, n_batch, TM, s_lat, 1, s_lat, 1)], axis=1)

        y_ret = _retention(zr, _retention_tables(ret_log_gamma[l].astype(F32)),
                           ret_norm_g[l].reshape(1, GROUP_W), n_batch, n_lat, ctx_out)

        s_hy, x0_hy = _hy_pre(zh, hy_conv_w[l], hy_conv_b[l].reshape(1, Z_HY), n_batch, n_lat)
        w1p = jnp.pad(hy_w1[l], ((0, LANES - hy_w1.shape[1]), (0, 0)))
        fargs = (w1p, hy_b1[l].reshape(1, -1), hy_w2[l], hy_b2[l].reshape(1, -1), hy_w3[l],
                 hy_freq[l].reshape(1, -1))
        hf, hb = _hy_filter(s_lat, *fargs)
        s_t = jnp.transpose(s_hy[:, :s_lat].reshape(n_batch, n_lat, TM, GROUP_W), (3, 1, 0, 2))
        y_t = _hy_conv(s_t, _toeplitz_rows(hf, hb, n_lat))
        y_hy = jnp.transpose(y_t, (2, 1, 3, 0)).reshape(n_batch, s_lat, GROUP_W)
        if ctx_out:
            hf_c, hb_c = _hy_filter(n_ctx, *fargs)
            sc_t = jnp.transpose(s_hy[:, s_lat:].reshape(n_batch, 1, TM, GROUP_W), (3, 1, 0, 2))
            yc_t = _hy_conv(sc_t, _toeplitz_rows(hf_c, hb_c, 1))
            y_hy_c = jnp.transpose(yc_t, (2, 1, 3, 0)).reshape(n_batch, n_ctx, GROUP_W)
        else:
            y_hy_c = jnp.zeros((n_batch, n_ctx, GROUP_W), BF16)
        y_hy = jnp.concatenate([y_hy, y_hy_c], axis=1)

        wg = jnp.concatenate([_block_diag(lru_wa[l, 0]), _block_diag(lru_wx[l, 0]),
                              _block_diag(lru_wa[l, 1]), _block_diag(lru_wx[l, 1])], axis=1).astype(BF16)
        bg = jnp.concatenate([lru_ba[l, 0], lru_bx[l, 0], lru_ba[l, 1], lru_bx[l, 1]]).reshape(1, -1)
        sp = jax.nn.softplus(-lru_lambda[l].astype(F32))
        y_lru = _lru(zl, lru_conv_w[l], lru_conv_b[l].reshape(1, -1), wg, bg, sp, n_batch, n_lat)

        wr = jnp.pad(jnp.concatenate([moe_w_group[l], moe_w_expert[l]], axis=1),
                     ((0, 0), (0, LANES - MOE_GROUPS - N_EXPERTS))).astype(BF16)
        xx, f, route = _merge(src, (y_mla, y_ret, y_hy, s_hy, x0_hy, y_lru), hy_d[l].reshape(1, -1),
                               group_norm_g[l].reshape(1, -1), w_out[l].astype(BF16), mod,
                               norm2_g[l].reshape(1, d), wr, n_batch, nt_all, nt, n_lat)

        ya, yb = _moe(f, route, moe_w1, moe_w3, moe_w2, l)
        xx = _combine(xx, ya, yb, route, mod, n_batch, nt_all, nt, n_lat, in_place=ctx_out)
        src = (xx,)

    return xx.reshape(n_batch, s_lat, d)
```

```python
import functools
import math

import jax
import jax.numpy as jnp
from jax import lax
from jax.experimental import pallas as pl
from jax.experimental.pallas import tpu as pltpu

F32 = jnp.float32
BF16 = jnp.bfloat16

D_MODEL = 1024
EPS = 1e-6
GRID_W = 64
N_GROUPS = 4
GROUP_W = 256

MLA_HEADS = 4
MLA_NOPE = 64
MLA_ROPE = 32
MLA_QK = 96
MLA_V = 64
MLA_Q_RANK = 192
MLA_KV_RANK = 128
ROPE_BASE = 10000.0

RET_HEADS = 4
RET_DK = 64

HY_BANDS = 16
HY_FAST_PCT = 0.3
HY_SLOW_PCT = 1.5
HY_TARGET = 1e-2

LRU_C = 8.0

MOE_GROUPS = 4
MOE_PER_GROUP = 8
N_EXPERTS = 32
TOP_K = 2
EXPERT_FF = 512

LANES = 128
SUBLANES = 8
MXU_DIM = 256

TM = 256
MLA_TQ = 512
MOE_BM = 512
HY_CB = 8
SCAN_UNROLL = 8
RET_UNROLL = 4
VMEM_LIMIT = 56 * 1024 * 1024

Z_MLA, Z_RET, Z_HY, Z_LRU = 384, 1024, 768, 512
Z_COLS = Z_MLA + Z_RET + Z_HY + Z_LRU


def _cparams(sem):
    return pltpu.CompilerParams(dimension_semantics=sem, vmem_limit_bytes=VMEM_LIMIT)


def _bdot(a, b):
    return jnp.dot(a.astype(BF16), b.astype(BF16), preferred_element_type=F32)


def _bdot_t(a, b):
    return lax.dot_general(a.astype(BF16), b.astype(BF16), (((1,), (1,)), ((), ())),
                           preferred_element_type=F32)


def _fdot(a, b):
    return jnp.dot(a, b, preferred_element_type=F32, precision=lax.Precision.HIGHEST)


def _sigmoid(x):
    return 0.5 * jnp.tanh(0.5 * x) + 0.5


def _shift_rows(x, d):
    n = x.shape[0]
    r = pltpu.roll(x, (n - d) % n, 0)
    row = lax.broadcasted_iota(jnp.int32, (n, 1), 0)
    ok = (row + d >= 0) & (row + d < n)
    return jnp.where(ok, r, 0.0)


def _mod_kernel(c_ref, w_ref, b_ref, o_ref):
    c = c_ref[...]
    a = c * jax.nn.sigmoid(c)
    o_ref[0] = _fdot(a, w_ref[0]) + b_ref[0]


def _modulation(cc, w_mod, b_mod):
    nl, d, n6 = w_mod.shape
    rows = cc.shape[0]
    tn = 1536
    return pl.pallas_call(
        _mod_kernel,
        out_shape=jax.ShapeDtypeStruct((nl, rows, n6), F32),
        grid=(nl, n6 // tn),
        in_specs=[pl.BlockSpec((rows, d), lambda l, j: (0, 0)),
                  pl.BlockSpec((1, d, tn), lambda l, j: (l, 0, j)),
                  pl.BlockSpec((1, 1, tn), lambda l, j: (l, 0, j))],
        out_specs=pl.BlockSpec((1, rows, tn), lambda l, j: (l, 0, j)),
        compiler_params=_cparams(("arbitrary", "arbitrary")),
        name="modulation",
    )(cc, w_mod, b_mod.reshape(nl, 1, n6))


def _stream_specs(src, nt_all, n_lat):
    if len(src) == 1:
        return [pl.BlockSpec((TM, D_MODEL), lambda b, t: (b * nt_all + t, 0))]
    return [pl.BlockSpec((1, TM, D_MODEL), lambda b, t: (b, jnp.minimum(t, n_lat - 1), 0)),
            pl.BlockSpec((1, TM, D_MODEL), lambda b, t: (b, 0, 0))]


def _stream_tile(refs, n_lat):
    if len(refs) == 1:
        return refs[0][...]
    return jnp.where(pl.program_id(1) < n_lat, refs[0][0], refs[1][0])


def _inproj_kernel(n_src, n_lat, *refs):
    x = _stream_tile(refs[:n_src], n_lat)
    mod_ref, g_ref, w_ref, zm_ref, zr_ref, zh_ref, zl_ref = refs[n_src:]
    ms = jnp.mean(x * x, axis=-1, keepdims=True)
    y = x * lax.rsqrt(ms + EPS) * g_ref[...]
    sh = mod_ref[0, :, 0:D_MODEL]
    sc = mod_ref[0, :, D_MODEL:2 * D_MODEL]
    a = y * (1.0 + sc) + sh
    z = _bdot(a, w_ref[...])
    o = 0
    for ref, w in ((zm_ref, Z_MLA), (zr_ref, Z_RET), (zh_ref, Z_HY), (zl_ref, Z_LRU)):
        ref[...] = z[:, o:o + w].astype(BF16)
        o += w


def _mod_index(nt_all, n_lat, n_batch):
    def idx(b, t):
        return (jnp.where(t < n_lat, b, n_batch), 0, 0)
    return idx


def _inproj(src, mod, g1, w_in_p, n_batch, nt, n_lat):
    rows = n_batch * nt * TM
    row_map = lambda b, t: (b * nt + t, 0)
    outs = [jax.ShapeDtypeStruct((rows, w), BF16) for w in (Z_MLA, Z_RET, Z_HY, Z_LRU)]
    return pl.pallas_call(
        functools.partial(_inproj_kernel, len(src), n_lat),
        out_shape=outs,
        grid=(n_batch, nt),
        in_specs=_stream_specs(src, nt, n_lat)
                 + [pl.BlockSpec((1, 1, 6 * D_MODEL), _mod_index(nt, n_lat, n_batch)),
                    pl.BlockSpec((1, D_MODEL), lambda b, t: (0, 0)),
                    pl.BlockSpec((D_MODEL, Z_COLS), lambda b, t: (0, 0))],
        out_specs=[pl.BlockSpec((TM, w), row_map) for w in (Z_MLA, Z_RET, Z_HY, Z_LRU)],
        compiler_params=_cparams(("parallel", "arbitrary")),
        name="inproj",
    )(*src, mod, g1, w_in_p)


def _mla_kernel(tq, q_row0, key_row0, n_key_tiles, z_ref, cos_ref, sin_ref, gq_ref, gkv_ref, gqm_ref,
                gqs_ref, gkm_ref, gkr_ref, gks_ref, wq_ref, wkv_ref, o_ref, k_scr, v_scr):
    t = pl.program_id(1)
    hw = MLA_HEADS * LANES
    lane = lax.broadcasted_iota(jnp.int32, (1, LANES), 1)
    is_rope = (lane >= MLA_NOPE) & (lane < MLA_QK)
    vlane = lax.broadcasted_iota(jnp.int32, (1, GROUP_W), 1)

    def kv_tile(i, carry):
        r0 = pl.multiple_of(i * TM, TM)
        rs0 = pl.multiple_of(key_row0 + i * TM, TM)
        z = z_ref[0, pl.ds(rs0, TM), LANES:3 * LANES].astype(F32)
        col1 = z[:, 0:LANES]
        zb = z[:, LANES:2 * LANES]
        cos = cos_ref[pl.ds(rs0, TM), :]
        sin = sin_ref[pl.ds(rs0, TM), :]
        ms_kv = jnp.mean(zb * zb, axis=-1, keepdims=True)
        kv = _bdot(zb * lax.rsqrt(ms_kv + EPS) * gkv_ref[...], wkv_ref[...])
        kr = jnp.where(is_rope, col1, 0.0)
        kr_sw = jnp.where(is_rope, pltpu.roll(col1, LANES - MLA_ROPE, 1), 0.0)
        k_rot = kr * gkr_ref[...] * cos + kr_sw * gks_ref[...] * sin
        ss_kr = jnp.sum(kr * kr, axis=-1, keepdims=True)
        for h in range(MLA_HEADS):
            kn = kv[:, h * LANES:(h + 1) * LANES]
            rk = lax.rsqrt((jnp.sum(kn * kn, axis=-1, keepdims=True) + ss_kr) / MLA_QK + EPS)
            k_scr[pl.ds(r0, TM), h * LANES:(h + 1) * LANES] = (
                rk * (kn * gkm_ref[...] + k_rot)).astype(BF16)
            vh = kv[:, hw + h * GROUP_W:hw + (h + 1) * GROUP_W]
            ones_lane = ((h + 1) % MLA_HEADS) * MLA_V
            v_scr[pl.ds(r0, TM), h * GROUP_W:(h + 1) * GROUP_W] = jnp.where(
                vlane == ones_lane, 1.0, vh).astype(BF16)
        return carry

    @pl.when(t == 0)
    def _():
        lax.fori_loop(0, n_key_tiles, kv_tile, 0)

    q0 = pl.multiple_of(q_row0 + t * tq, TM)
    za = z_ref[0, pl.ds(q0, tq), 0:2 * LANES].astype(F32)
    cos = cos_ref[pl.ds(q0, tq), :]
    sin = sin_ref[pl.ds(q0, tq), :]
    lane2 = lax.broadcasted_iota(jnp.int32, (1, 2 * LANES), 1)
    ms_q = jnp.sum(jnp.where(lane2 < MLA_Q_RANK, za * za, 0.0), axis=-1, keepdims=True) / MLA_Q_RANK
    qall = _bdot(za * lax.rsqrt(ms_q + EPS) * gq_ref[...], wq_ref[...])
    q_heads = []
    for h in range(MLA_HEADS):
        qh = qall[:, h * LANES:(h + 1) * LANES]
        qs = qall[:, hw + h * LANES:hw + (h + 1) * LANES]
        rs = lax.rsqrt(jnp.sum(qh * qh, axis=-1, keepdims=True) / MLA_QK + EPS) * (MLA_QK ** -0.5)
        q_heads.append((rs * (qh * gqm_ref[...] * cos + qs * gqs_ref[...] * sin)).astype(BF16))

    acc = jnp.zeros((tq, GROUP_W), F32)
    for h in range(MLA_HEADS):
        s = _bdot_t(q_heads[h], k_scr[:, h * LANES:(h + 1) * LANES])
        m = jnp.max(s, axis=-1, keepdims=True)
        p = jnp.exp((s - m).astype(BF16))
        pv = jnp.dot(p, v_scr[:, h * GROUP_W:(h + 1) * GROUP_W], preferred_element_type=F32)
        ones_lane = ((h + 1) % MLA_HEADS) * MLA_V
        l = jnp.sum(jnp.where(vlane == ones_lane, pv, 0.0), axis=-1, keepdims=True)
        in_head = (vlane >= h * MLA_V) & (vlane < (h + 1) * MLA_V)
        acc = acc + jnp.where(in_head, pv * (1.0 / l), 0.0)
    o_ref[0] = acc.astype(BF16)


def _mla(zm, cos_t, sin_t, gains, wq, wkv, n_batch, tq, q_row0, n_q_tiles, key_row0, n_key_tiles):
    seq = zm.shape[1]
    hw = MLA_HEADS * LANES
    vw = MLA_HEADS * GROUP_W
    n_keys = n_key_tiles * TM
    small = lambda w: pl.BlockSpec((1, w), lambda b, t: (0, 0))
    return pl.pallas_call(
        functools.partial(_mla_kernel, tq, q_row0, key_row0, n_key_tiles),
        out_shape=jax.ShapeDtypeStruct((n_batch, n_q_tiles * tq, GROUP_W), BF16),
        grid=(n_batch, n_q_tiles),
        in_specs=[pl.BlockSpec((1, seq, Z_MLA), lambda b, t: (b, 0, 0)),
                  pl.BlockSpec((seq, LANES), lambda b, t: (0, 0)),
                  pl.BlockSpec((seq, LANES), lambda b, t: (0, 0)),
                  small(2 * LANES), small(LANES), small(LANES), small(LANES), small(LANES),
                  small(LANES), small(LANES),
                  pl.BlockSpec((2 * LANES, 2 * hw), lambda b, t: (0, 0)),
                  pl.BlockSpec((LANES, hw + vw), lambda b, t: (0, 0))],
        out_specs=pl.BlockSpec((1, tq, GROUP_W), lambda b, t: (b, t, 0)),
        scratch_shapes=[pltpu.VMEM((n_keys, hw), BF16), pltpu.VMEM((n_keys, vw), BF16)],
        compiler_params=_cparams(("parallel", "arbitrary")),
        name="mla_attention",
    )(zm, cos_t, sin_t, *gains, wq, wkv)


def _ret_kernel(n_lat, ctx_out, z_ref, dm_ref, dq_ref, dk_ref, gc_ref, bd_ref, g_ref,
                o_ref, acc_ref, st_ref):
    lane = lax.broadcasted_iota(jnp.int32, (1, GROUP_W), 1)
    hmask = [(lane >= h * RET_DK) & (lane < (h + 1) * RET_DK) for h in range(RET_HEADS)]
    ones_bd = bd_ref[...].astype(BF16)

    def qkv(r0):
        q = z_ref[0, pl.ds(r0, TM), 0:GROUP_W]
        k = z_ref[0, pl.ds(r0, TM), GROUP_W:2 * GROUP_W].astype(F32) * (RET_DK ** -0.5)
        v = z_ref[0, pl.ds(r0, TM), 2 * GROUP_W:3 * GROUP_W]
        return q, k, v

    def inner(q, k, v):
        kb = k.astype(BF16)
        acc = jnp.zeros((TM, GROUP_W), F32)
        for h in range(RET_HEADS):
            qh = jnp.where(hmask[h], q, jnp.zeros_like(q))
            vh = jnp.where(hmask[h], v, jnp.zeros_like(v))
            s = _bdot_t(qh, kb) * dm_ref[h]
            acc = acc + _bdot(s, vh)
        return acc

    def state_update(d, k, v):
        kd = (k * dk_ref[d]).T
        st_ref[...] = st_ref[...] * gc_ref[d] + _bdot(kd, v) * bd_ref[...]

    def cross(d, q):
        return _bdot(q.astype(F32) * dq_ref[d], st_ref[...])

    def finish(r0, o):
        sq = o * o
        hi = sq.astype(BF16)
        lo = (sq - hi.astype(F32)).astype(BF16)
        ms = (jnp.dot(hi, ones_bd, preferred_element_type=F32)
              + jnp.dot(lo, ones_bd, preferred_element_type=F32)) / RET_DK
        gate = z_ref[0, pl.ds(r0, TM), 3 * GROUP_W:4 * GROUP_W].astype(F32)
        y = o * lax.rsqrt(ms + EPS) * g_ref[...] * (gate * _sigmoid(gate))
        o_ref[0, pl.ds(r0, TM), :] = y.astype(BF16)

    c0 = n_lat * TM
    qc, kc, vc = qkv(c0)
    if ctx_out:
        finish(c0, inner(qc, kc, vc))
    else:
        o_ref[0, pl.ds(c0, TM), :] = jnp.zeros((TM, GROUP_W), BF16)

    st_ref[...] = jnp.zeros_like(st_ref)
    state_update(0, kc, vc)

    def fwd(n, carry):
        r0 = pl.multiple_of(n * TM, TM)
        q, k, v = qkv(r0)
        acc_ref[pl.ds(r0, TM), :] = inner(q, k, v) + cross(0, q)
        state_update(0, k, v)
        return carry

    lax.fori_loop(0, n_lat, fwd, 0, unroll=RET_UNROLL)

    st_ref[...] = jnp.zeros_like(st_ref)
    state_update(1, kc, vc)

    def bwd(i, carry):
        n = n_lat - 1 - i
        r0 = pl.multiple_of(n * TM, TM)
        q, k, v = qkv(r0)
        finish(r0, acc_ref[pl.ds(r0, TM), :] + cross(1, q))
        state_update(1, k, v)
        return carry

    lax.fori_loop(0, n_lat, bwd, 0, unroll=RET_UNROLL)


def _retention(zr, tabs, g, n_batch, n_lat, ctx_out):
    seq = zr.shape[1]
    dm, dq, dk, gc, bd = tabs
    full = lambda a: pl.BlockSpec(a.shape, lambda b: (0,) * a.ndim)
    return pl.pallas_call(
        functools.partial(_ret_kernel, n_lat, ctx_out),
        out_shape=jax.ShapeDtypeStruct((n_batch, seq, GROUP_W), BF16),
        grid=(n_batch,),
        in_specs=[pl.BlockSpec((1, seq, Z_RET), lambda b: (b, 0, 0)),
                  full(dm), full(dq), full(dk), full(gc), full(bd), full(g)],
        out_specs=pl.BlockSpec((1, seq, GROUP_W), lambda b: (b, 0, 0)),
        scratch_shapes=[pltpu.VMEM((n_lat * TM, GROUP_W), F32),
                        pltpu.VMEM((GROUP_W, GROUP_W), F32)],
        compiler_params=_cparams(("parallel",)),
        name="retention",
    )(zr, dm, dq, dk, gc, bd, g)


def _retention_tables(lg):
    j = jnp.arange(TM, dtype=F32)
    rel = j[:, None] - j[None, :]
    lf = lg[0][:, None, None]
    lb = lg[1][:, None, None]
    dm = (jnp.where(rel[None] >= 0, jnp.exp(jnp.maximum(rel, 0.0)[None] * lf), 0.0)
          + jnp.where(rel[None] <= 0, jnp.exp(jnp.maximum(-rel, 0.0)[None] * lb), 0.0))
    lane_f = jnp.repeat(lg[0], RET_DK)[None, :]
    lane_b = jnp.repeat(lg[1], RET_DK)[None, :]
    dq = jnp.stack([jnp.exp((j + 1)[:, None] * lane_f), jnp.exp((TM - j)[:, None] * lane_b)])
    dk = jnp.stack([jnp.exp((TM - 1 - j)[:, None] * lane_f), jnp.exp(j[:, None] * lane_b)])
    gc = jnp.stack([jnp.exp(TM * lane_f).reshape(GROUP_W, 1), jnp.exp(TM * lane_b).reshape(GROUP_W, 1)])
    hid = jnp.arange(GROUP_W) // RET_DK
    bd = (hid[:, None] == hid[None, :]).astype(F32)
    return dm, dq, dk, gc, bd


def _hy_pre_kernel(n_lat, z_ref, w_ref, b_ref, s_ref, x0_ref):
    def seg(r0, n):
        z = z_ref[0, r0:r0 + n, :].astype(F32)
        u = (_shift_rows(z, -1) * w_ref[0:1, :] + z * w_ref[1:2, :]
             + _shift_rows(z, 1) * w_ref[2:3, :] + b_ref[...])
        x0_ref[0, r0:r0 + n, :] = u[:, 0:GROUP_W].astype(BF16)
        s_ref[0, r0:r0 + n, :] = (u[:, GROUP_W:2 * GROUP_W] * u[:, 2 * GROUP_W:]).astype(BF16)

    seg(0, n_lat * TM)
    seg(n_lat * TM, z_ref.shape[1] - n_lat * TM)


def _hy_pre(zh, w, b, n_batch, n_lat):
    seq = zh.shape[1]
    return pl.pallas_call(
        functools.partial(_hy_pre_kernel, n_lat),
        out_shape=[jax.ShapeDtypeStruct((n_batch, seq, GROUP_W), BF16)] * 2,
        grid=(n_batch,),
        in_specs=[pl.BlockSpec((1, seq, Z_HY), lambda i: (i, 0, 0)),
                  pl.BlockSpec(w.shape, lambda i: (0, 0)),
                  pl.BlockSpec(b.shape, lambda i: (0, 0))],
        out_specs=[pl.BlockSpec((1, seq, GROUP_W), lambda i: (i, 0, 0))] * 2,
        compiler_params=_cparams(("parallel",)),
        name="hyena_pre",
    )(zh, w, b)


def _hy_filter_kernel(z_ref, win_ref, w1_ref, b1_ref, w2_ref, b2_ref, w3_ref, fr_ref, hf_ref, hb_ref):
    fr = fr_ref[...]
    h = jnp.sin(fr * (_fdot(z_ref[...], w1_ref[...]) + b1_ref[...]))
    h = jnp.sin(fr * (_fdot(h, w2_ref[...]) + b2_ref[...]))
    filt = _fdot(h, w3_ref[...])
    win = win_ref[...]
    hf_ref[...] = filt[:, 0:GROUP_W] * win
    hb_ref[...] = filt[:, GROUP_W:] * win


def _hy_filter(n, w1p, b1, w2, b2, w3, freq):
    t = jnp.linspace(0.0, 1.0, n, dtype=F32)[:, None]
    bands = jnp.linspace(1e-4, HY_BANDS - 1, HY_BANDS, dtype=F32)
    w = 2.0 * math.pi * jnp.arange(n, dtype=F32)[:, None] / n
    z = jnp.concatenate([t, jnp.cos(bands * w), -jnp.sin(bands * w)], axis=-1)
    z = jnp.pad(z, ((0, 0), (0, LANES - z.shape[1])))
    max_decay = math.log(HY_TARGET) / HY_FAST_PCT
    min_decay = math.log(HY_TARGET) / HY_SLOW_PCT
    deltas = jnp.abs(jnp.linspace(min_decay, max_decay, GROUP_W, dtype=F32))
    window = jnp.exp(-t * deltas)
    full = lambda a: pl.BlockSpec(a.shape, lambda i: (0, 0))
    rows = lambda wd: pl.BlockSpec((TM, wd), lambda i: (i, 0))
    return pl.pallas_call(
        _hy_filter_kernel,
        out_shape=[jax.ShapeDtypeStruct((n, GROUP_W), F32)] * 2,
        grid=(n // TM,),
        in_specs=[rows(LANES), rows(GROUP_W), full(w1p), full(b1), full(w2), full(b2), full(w3),
                  full(freq)],
        out_specs=[rows(GROUP_W)] * 2,
        compiler_params=_cparams(("arbitrary",)),
        name="hyena_filter",
    )(z, window, w1p, b1, w2, b2, w3, freq)


def _toeplitz_rows(hf, hb, nj):
    n = hf.shape[0]
    full = jnp.concatenate([hf, jnp.zeros((1, GROUP_W), F32), hb[1:][::-1]], axis=0)
    d = jnp.arange(-(nj - 1), nj)[:, None]
    m = jnp.arange(2 * TM)[None, :]
    lag = d * TM + jnp.where(m < TM, m, m - 2 * TM)
    idx = jnp.mod(lag, 2 * n)
    return jnp.transpose(full[idx], (2, 0, 1))


def _hy_conv_kernel(nj, nb, s_ref, w_ref, y_ref, acc_ref):
    def chan(c, carry):
        sc = s_ref[c].reshape(nj * nb, TM)
        acc_ref[...] = jnp.zeros_like(acc_ref)
        for di in range(2 * nj - 1):
            d = di - (nj - 1)
            w = w_ref[c, di:di + 1, :]
            rolled = pltpu.roll(jnp.broadcast_to(w, (TM, 2 * TM)), 0, 1, stride=1, stride_axis=0)
            toep = rolled[:, 0:TM].astype(BF16)
            if d >= 0:
                res = jnp.dot(sc[0:(nj - d) * nb], toep, preferred_element_type=F32)
                acc_ref[d * nb:nj * nb, :] += res
            else:
                res = jnp.dot(sc[(-d) * nb:nj * nb], toep, preferred_element_type=F32)
                acc_ref[0:(nj + d) * nb, :] += res
        y_ref[c] = acc_ref[...].reshape(nj, nb, TM).astype(BF16)
        return carry

    lax.fori_loop(0, HY_CB, chan, 0)


def _hy_conv(s_t, w_rows):
    ch, nj, nb, _ = s_t.shape
    nd = 2 * nj - 1
    return pl.pallas_call(
        functools.partial(_hy_conv_kernel, nj, nb),
        out_shape=jax.ShapeDtypeStruct(s_t.shape, BF16),
        grid=(ch // HY_CB,),
        in_specs=[pl.BlockSpec((HY_CB, nj, nb, TM), lambda i: (i, 0, 0, 0)),
                  pl.BlockSpec((HY_CB, nd, 2 * TM), lambda i: (i, 0, 0))],
        out_specs=pl.BlockSpec((HY_CB, nj, nb, TM), lambda i: (i, 0, 0, 0)),
        scratch_shapes=[pltpu.VMEM((nj * nb, TM), F32)],
        compiler_params=_cparams(("parallel",)),
        name="hyena_conv",
    )(s_t, w_rows)


def _lru_tile_scan(a, b, carry, reverse):
    row = lax.broadcasted_iota(jnp.int32, (SUBLANES, 1), 0)
    for k in (1, 2, 4):
        if reverse:
            ok = row < SUBLANES - k
            sh = SUBLANES - k
        else:
            ok = row >= k
            sh = k
        a_s = jnp.where(ok, pltpu.roll(a, sh, 0), 1.0)
        b_s = jnp.where(ok, pltpu.roll(b, sh, 0), 0.0)
        b = a * b_s + b
        a = a * a_s
    return b + a * carry


def _lru_kernel(n_lat, z_ref, cw_ref, cb_ref, wg_ref, bg_ref, sp_ref, o_ref,
                af_ref, bf_ref, ab_ref, bb_ref, h_ref):
    seq = z_ref.shape[1]
    s_lat = n_lat * TM

    def prep(r0, n):
        x = z_ref[0, r0:r0 + n, 0:GROUP_W].astype(F32)
        x = (_shift_rows(x, -2) * cw_ref[0:1, :] + _shift_rows(x, -1) * cw_ref[1:2, :]
             + x * cw_ref[2:3, :] + _shift_rows(x, 1) * cw_ref[3:4, :] + cb_ref[...])
        g = (_bdot(x, wg_ref[...]) + bg_ref[...]).astype(BF16)
        for d, (a_ref, b_ref) in enumerate(((af_ref, bf_ref), (ab_ref, bb_ref))):
            r = _sigmoid(g[:, (2 * d) * GROUP_W:(2 * d + 1) * GROUP_W]).astype(F32)
            i = _sigmoid(g[:, (2 * d + 1) * GROUP_W:(2 * d + 2) * GROUP_W]).astype(F32)
            a = jnp.exp(-LRU_C * r * sp_ref[d:d + 1, :])
            a_ref[r0:r0 + n, :] = a
            b_ref[r0:r0 + n, :] = jnp.sqrt(1.0 - a * a) * (i * x)

    prep(0, s_lat)
    prep(s_lat, seq - s_lat)

    n_ctx_tiles = (seq - s_lat) // SUBLANES
    n_lat_tiles = s_lat // SUBLANES

    def fwd_tile(base):
        def body(i, carry):
            r0 = pl.multiple_of(base + i * SUBLANES, SUBLANES)
            h = _lru_tile_scan(af_ref[pl.ds(r0, SUBLANES), :], bf_ref[pl.ds(r0, SUBLANES), :],
                               carry, False)
            h_ref[pl.ds(r0, SUBLANES), :] = h
            return h[SUBLANES - 1:SUBLANES, :]
        return body

    carry = jnp.zeros((1, GROUP_W), F32)
    carry = lax.fori_loop(0, n_ctx_tiles, fwd_tile(s_lat), carry, unroll=SCAN_UNROLL)
    lax.fori_loop(0, n_lat_tiles, fwd_tile(0), carry, unroll=SCAN_UNROLL)

    def bwd_tile(base, n_tiles):
        def body(i, carry):
            r0 = pl.multiple_of(base + (n_tiles - 1 - i) * SUBLANES, SUBLANES)
            h = _lru_tile_scan(ab_ref[pl.ds(r0, SUBLANES), :], bb_ref[pl.ds(r0, SUBLANES), :],
                               carry, True)
            gate = z_ref[0, pl.ds(r0, SUBLANES), GROUP_W:2 * GROUP_W].astype(F32)
            y = (h + h_ref[pl.ds(r0, SUBLANES), :]) * jax.nn.gelu(gate, approximate=True)
            o_ref[0, pl.ds(r0, SUBLANES), :] = y.astype(BF16)
            return h[0:1, :]
        return body

    carry = jnp.zeros((1, GROUP_W), F32)
    carry = lax.fori_loop(0, n_ctx_tiles, bwd_tile(s_lat, n_ctx_tiles), carry, unroll=SCAN_UNROLL)
    lax.fori_loop(0, n_lat_tiles, bwd_tile(0, n_lat_tiles), carry, unroll=SCAN_UNROLL)


def _lru(zl, cw, cb, wg, bg, sp, n_batch, n_lat):
    seq = zl.shape[1]
    full = lambda a: pl.BlockSpec(a.shape, lambda i: (0,) * a.ndim)
    return pl.pallas_call(
        functools.partial(_lru_kernel, n_lat),
        out_shape=jax.ShapeDtypeStruct((n_batch, seq, GROUP_W), BF16),
        grid=(n_batch,),
        in_specs=[pl.BlockSpec((1, seq, Z_LRU), lambda i: (i, 0, 0)),
                  full(cw), full(cb), full(wg), full(bg), full(sp)],
        out_specs=pl.BlockSpec((1, seq, GROUP_W), lambda i: (i, 0, 0)),
        scratch_shapes=[pltpu.VMEM((seq, GROUP_W), F32)] * 5,
        compiler_params=_cparams(("parallel",)),
        name="rglru",
    )(zl, cw, cb, wg, bg, sp)


def _merge_kernel(n_src, n_lat, *refs):
    x_in = _stream_tile(refs[:n_src], n_lat)
    (ya_ref, yb_ref, yc_ref, s_ref, x0_ref, yd_ref, hd_ref, gg_ref, wo_ref,
     mod_ref, g2_ref, wr_ref, x1_ref, f_ref, rt_ref, cat_ref) = refs[n_src:]
    s = s_ref[0].astype(F32)
    yh = x0_ref[0].astype(F32) * (yc_ref[0].astype(F32) + s * hd_ref[...])
    parts = (ya_ref[0].astype(F32), yb_ref[0].astype(F32), yh, yd_ref[0].astype(F32))
    for g, y in enumerate(parts):
        ms = jnp.mean(y * y, axis=-1, keepdims=True)
        gg = gg_ref[:, g * GROUP_W:(g + 1) * GROUP_W]
        cat_ref[:, g * GROUP_W:(g + 1) * GROUP_W] = (y * lax.rsqrt(ms + EPS) * gg).astype(BF16)
    m = jnp.dot(cat_ref[...], wo_ref[...], preferred_element_type=F32)
    gate1 = mod_ref[0, :, 2 * D_MODEL:3 * D_MODEL]
    x1 = x_in + gate1 * m
    x1_ref[...] = x1
    ms = jnp.mean(x1 * x1, axis=-1, keepdims=True)
    sh = mod_ref[0, :, 3 * D_MODEL:4 * D_MODEL]
    sc = mod_ref[0, :, 4 * D_MODEL:5 * D_MODEL]
    f = (x1 * lax.rsqrt(ms + EPS) * g2_ref[...]) * (1.0 + sc) + sh
    fb = f.astype(BF16)
    f_ref[...] = fb
    rt_ref[...] = _route(jnp.dot(fb, wr_ref[...], preferred_element_type=F32))


def _route(logits):
    lane = lax.broadcasted_iota(jnp.int32, logits.shape, 1)
    neg = jnp.float32(-jnp.inf)
    big = jnp.int32(LANES)
    gl = jnp.where(lane < MOE_GROUPS, logits, neg)
    gmax = jnp.max(gl, axis=-1, keepdims=True)
    gsum = jnp.sum(jnp.exp(gl - gmax), axis=-1, keepdims=True)
    g_val = 1.0 / gsum
    g_idx = jnp.min(jnp.where(gl == gmax, lane, big), axis=-1, keepdims=True)
    lo = MOE_GROUPS + MOE_PER_GROUP * g_idx
    el = jnp.where((lane >= lo) & (lane < lo + MOE_PER_GROUP), logits, neg)
    m1 = jnp.max(el, axis=-1, keepdims=True)
    i1 = jnp.min(jnp.where(el == m1, lane, big), axis=-1, keepdims=True)
    el2 = jnp.where(lane == i1, neg, el)
    m2 = jnp.max(el2, axis=-1, keepdims=True)
    i2 = jnp.min(jnp.where(el2 == m2, lane, big), axis=-1, keepdims=True)
    r = jnp.exp(m2 - m1)
    w1 = g_val / (1.0 + r)
    w2 = w1 * r
    e1 = (i1 - MOE_GROUPS).astype(F32)
    e2 = (i2 - MOE_GROUPS).astype(F32)
    return jnp.where(lane == 0, e1, jnp.where(lane == 1, e2, jnp.where(lane == 2, w1,
                     jnp.where(lane == 3, w2, 0.0))))


def _merge(src, ys, hd, gg, wo, mod, g2, wr, n_batch, nt_all, nt, n_lat):
    rows_all = n_batch * nt_all * TM
    rows_out = n_batch * nt * TM
    in_rows = lambda b, t: (b * nt_all + t, 0)
    out_rows = lambda b, t: (b * nt + t, 0)
    seq_blk = pl.BlockSpec((1, TM, GROUP_W), lambda b, t: (b, t, 0))
    small = lambda a: pl.BlockSpec(a.shape, lambda b, t: (0, 0))
    return pl.pallas_call(
        functools.partial(_merge_kernel, len(src), n_lat),
        out_shape=[jax.ShapeDtypeStruct((rows_all, D_MODEL), F32),
                   jax.ShapeDtypeStruct((rows_out, D_MODEL), BF16),
                   jax.ShapeDtypeStruct((rows_out, LANES), F32)],
        grid=(n_batch, nt),
        in_specs=_stream_specs(src, nt_all, n_lat) + [seq_blk] * 6
                 + [small(hd), small(gg), small(wo),
                    pl.BlockSpec((1, 1, 6 * D_MODEL), _mod_index(nt_all, n_lat, n_batch)),
                    small(g2), small(wr)],
        out_specs=[pl.BlockSpec((TM, D_MODEL), in_rows),
                   pl.BlockSpec((TM, D_MODEL), out_rows),
                   pl.BlockSpec((TM, LANES), out_rows)],
        scratch_shapes=[pltpu.VMEM((TM, D_MODEL), BF16)],
        input_output_aliases={0: 0} if len(src) == 1 else {},
        compiler_params=_cparams(("parallel", "arbitrary")),
        name="merge_outproj",
    )(*src, *ys, hd, gg, wo, mod, g2, wr)


def _expert_kernel(be_ref, nv_ref, x_ref, w1_ref, w3_ref, w2_ref, o_ref, w1b_ref, w3b_ref, w2b_ref):
    i = pl.program_id(0)

    @pl.when((i == 0) | (be_ref[i] != be_ref[jnp.maximum(i - 1, 0)]))
    def _():
        w1b_ref[...] = w1_ref[0, 0].astype(BF16)
        w3b_ref[...] = w3_ref[0, 0].astype(BF16)
        w2b_ref[...] = w2_ref[0, 0].astype(BF16)

    @pl.when(i < nv_ref[0])
    def _():
        x = x_ref[...]
        a = jnp.dot(x, w1b_ref[...], preferred_element_type=F32)
        b = jnp.dot(x, w3b_ref[...], preferred_element_type=F32)
        h = (a * _sigmoid(a)) * b
        o_ref[...] = jnp.dot(h.astype(BF16), w2b_ref[...], preferred_element_type=F32).astype(BF16)

    @pl.when(i >= nv_ref[0])
    def _():
        o_ref[...] = jnp.zeros_like(o_ref)


def _experts(xs, block_exp, n_valid, w1, w3, w2, layer):
    n_slots = xs.shape[0]
    n_blocks = n_slots // MOE_BM
    w_idx = lambda i, be, nv: (layer, be[i], 0, 0)
    grid_spec = pltpu.PrefetchScalarGridSpec(
        num_scalar_prefetch=2,
        grid=(n_blocks,),
        in_specs=[pl.BlockSpec((MOE_BM, D_MODEL), lambda i, be, nv: (i, 0)),
                  pl.BlockSpec((1, 1, D_MODEL, EXPERT_FF), w_idx),
                  pl.BlockSpec((1, 1, D_MODEL, EXPERT_FF), w_idx),
                  pl.BlockSpec((1, 1, EXPERT_FF, D_MODEL), w_idx)],
        out_specs=pl.BlockSpec((MOE_BM, D_MODEL), lambda i, be, nv: (i, 0)),
        scratch_shapes=[pltpu.VMEM((D_MODEL, EXPERT_FF), BF16),
                        pltpu.VMEM((D_MODEL, EXPERT_FF), BF16),
                        pltpu.VMEM((EXPERT_FF, D_MODEL), BF16)],
    )
    return pl.pallas_call(
        _expert_kernel,
        out_shape=jax.ShapeDtypeStruct((n_slots, D_MODEL), BF16),
        grid_spec=grid_spec,
        compiler_params=_cparams(("arbitrary",)),
        name="moe_experts",
    )(block_exp, n_valid, xs, w1, w3, w2)


def _combine_kernel(x_ref, ya_ref, yb_ref, rt_ref, mod_ref, o_ref):
    gate2 = mod_ref[0, :, 5 * D_MODEL:6 * D_MODEL]
    ya = ya_ref[...].astype(F32)
    yb = yb_ref[...].astype(F32)
    o_ref[...] = x_ref[...] + gate2 * (rt_ref[:, 2:3] * ya + rt_ref[:, 3:4] * yb)


def _combine(xx, ya, yb, route, mod, n_batch, nt_all, nt, n_lat, in_place):
    in_rows = lambda b, t: (b * nt_all + t, 0)
    out_rows = lambda b, t: (b * nt + t, 0)
    rows = xx.shape[0] if in_place else n_batch * nt * TM
    return pl.pallas_call(
        _combine_kernel,
        out_shape=jax.ShapeDtypeStruct((rows, D_MODEL), F32),
        grid=(n_batch, nt),
        in_specs=[pl.BlockSpec((TM, D_MODEL), in_rows),
                  pl.BlockSpec((TM, D_MODEL), out_rows),
                  pl.BlockSpec((TM, D_MODEL), out_rows),
                  pl.BlockSpec((TM, LANES), out_rows),
                  pl.BlockSpec((1, 1, 6 * D_MODEL), _mod_index(nt_all, n_lat, n_batch))],
        out_specs=pl.BlockSpec((TM, D_MODEL), in_rows if in_place else out_rows),
        input_output_aliases={0: 0} if in_place else {},
        compiler_params=_cparams(("parallel", "arbitrary")),
        name="moe_combine",
    )(xx, ya, yb, route, mod)


def _dispatch_plan(eid):
    t = eid.shape[0]
    n_assign = t * TOP_K
    key_bits = (n_assign - 1).bit_length()
    assert key_bits + (N_EXPERTS - 1).bit_length() <= 31
    e_flat = eid.reshape(-1)
    experts = jnp.arange(N_EXPERTS, dtype=jnp.int32)
    keys = lax.sort((e_flat << key_bits) | jnp.arange(n_assign, dtype=jnp.int32), is_stable=False)
    a_sorted = keys & ((1 << key_bits) - 1)
    e_sorted = keys >> key_bits
    counts = jnp.sum((e_flat[:, None] == experts[None, :]).astype(jnp.int32), axis=0)
    padded = (counts + MOE_BM - 1) // MOE_BM * MOE_BM
    pad_end = jnp.cumsum(padded)
    shift = (pad_end - padded) - (jnp.cumsum(counts) - counts)
    n_blocks = -(-n_assign // MOE_BM) + N_EXPERTS
    n_slots = n_blocks * MOE_BM
    blk0 = jnp.arange(n_blocks, dtype=jnp.int32) * MOE_BM
    block_exp = jnp.minimum(jnp.sum((pad_end[None, :] <= blk0[:, None]).astype(jnp.int32), axis=1),
                            N_EXPERTS - 1)
    n_valid = (pad_end[-1] // MOE_BM).astype(jnp.int32).reshape(1)
    block_shift = jnp.sum(jnp.where(block_exp[:, None] == experts[None, :], shift[None, :], 0), axis=1)
    src = jnp.arange(n_slots, dtype=jnp.int32) - jnp.repeat(block_shift, MOE_BM)
    slot_tok = a_sorted.at[jnp.clip(src, 0, n_assign - 1)].get(mode='promise_in_bounds') // TOP_K
    dest = (jnp.arange(n_assign, dtype=jnp.int32)
            + jnp.sum(jnp.where(e_sorted[:, None] == experts[None, :], shift[None, :], 0), axis=1))
    _, pos = lax.sort((a_sorted, dest), num_keys=1)
    return slot_tok, pos.reshape(t, TOP_K), block_exp, n_valid


def _take_rows(a, idx):
    return a.at[idx].get(mode='promise_in_bounds')


def _moe(f, route, w1, w3, w2, layer):
    eid = route[:, 0:TOP_K].astype(jnp.int32)
    slot_tok, pos, block_exp, n_valid = _dispatch_plan(eid)
    ys = _experts(_take_rows(f, slot_tok), block_exp, n_valid, w1, w3, w2, layer)
    return _take_rows(ys, pos[:, 0]), _take_rows(ys, pos[:, 1])


def _pack_w_in(w_in):
    o = 0
    cols = {}
    for name, wd in (('cq', 192), ('ckv', 128), ('kr', 32), ('ret', 1024), ('hy', 768), ('lru', 512)):
        cols[name] = w_in[:, o:o + wd]
        o += wd
    swap = jnp.arange(MLA_ROPE) ^ 8
    kr_sw = cols['kr'][:, swap]
    return jnp.concatenate([cols['cq'], cols['kr'], kr_sw, cols['ckv'], cols['ret'], cols['hy'],
                            cols['lru']], axis=1).astype(BF16)


def _pack_mla(w_uq, w_ukv, q_norm_g, kv_norm_g, qn_g, kn_g):
    hw = MLA_HEADS * LANES
    swap = jnp.arange(MLA_ROPE) ^ 8
    eye = jnp.eye(MLA_HEADS, dtype=F32)
    q3 = w_uq.reshape(MLA_Q_RANK, MLA_HEADS, MLA_QK)
    q_main = jnp.pad(q3, ((0, 0), (0, 0), (0, LANES - MLA_QK)))
    q_swap = jnp.pad(q3[:, :, MLA_NOPE + swap], ((0, 0), (0, 0), (MLA_NOPE, LANES - MLA_QK)))
    wq = jnp.concatenate([q_main.reshape(MLA_Q_RANK, hw), q_swap.reshape(MLA_Q_RANK, hw)], axis=1)
    wq = jnp.pad(wq, ((0, 2 * LANES - MLA_Q_RANK), (0, 0)))
    kv3 = w_ukv.reshape(MLA_KV_RANK, MLA_HEADS, MLA_NOPE + MLA_V)
    k_part = jnp.pad(kv3[:, :, :MLA_NOPE], ((0, 0), (0, 0), (0, LANES - MLA_NOPE)))
    v_part = kv3[:, :, None, MLA_NOPE:] * eye[None, :, :, None]
    wkv = jnp.concatenate([k_part.reshape(MLA_KV_RANK, hw),
                           v_part.reshape(MLA_KV_RANK, MLA_HEADS * GROUP_W)], axis=1)
    pad = lambda v, lo, n: jnp.pad(v, (lo, n - lo - v.shape[0])).reshape(1, n)
    gains = (pad(q_norm_g, 0, 2 * LANES), kv_norm_g.reshape(1, LANES),
             pad(qn_g, 0, LANES), pad(qn_g[MLA_NOPE + swap], MLA_NOPE, LANES),
             pad(kn_g[:MLA_NOPE], 0, LANES), pad(kn_g[MLA_NOPE:], MLA_NOPE, LANES),
             pad(kn_g[MLA_NOPE + swap], MLA_NOPE, LANES))
    return wq.astype(BF16), wkv.astype(BF16), gains


def _rope_tables(s_lat, seq):
    rows = s_lat // GRID_W
    row = jnp.repeat(jnp.arange(rows), GRID_W).astype(F32)
    col = jnp.tile(jnp.arange(GRID_W), rows).astype(F32)
    half = MLA_ROPE // 4
    inv_freq = ROPE_BASE ** (-jnp.arange(half, dtype=F32) / half)
    ar = row[:, None] * inv_freq
    ac = col[:, None] * inv_freq
    cos32 = jnp.concatenate([jnp.cos(ar), jnp.cos(ar), jnp.cos(ac), jnp.cos(ac)], axis=1)
    sin32 = jnp.concatenate([-jnp.sin(ar), jnp.sin(ar), -jnp.sin(ac), jnp.sin(ac)], axis=1)
    lanes = ((MLA_NOPE, LANES - MLA_QK),)
    cos_t = jnp.pad(jnp.pad(cos32, ((0, 0),) + lanes, constant_values=1.0),
                    ((0, seq - s_lat), (0, 0)), constant_values=1.0)
    sin_t = jnp.pad(sin32, ((0, seq - s_lat),) + lanes)
    return cos_t, sin_t


def _block_diag(w):
    nb, bw, _ = w.shape
    eye = jnp.eye(nb, dtype=w.dtype)
    return (w[:, :, None, :] * eye[:, None, :, None]).reshape(nb * bw, nb * bw)


def kernel(x, c, ctx, c_ctx, w_mod, b_mod, norm1_g, norm2_g, w_in, mla_q_norm_g, mla_w_uq, mla_kv_norm_g, mla_w_ukv, mla_qn_g, mla_kn_g, ret_log_gamma, ret_norm_g, hy_conv_w, hy_conv_b, hy_w1, hy_b1, hy_w2, hy_b2, hy_w3, hy_freq, hy_d, lru_conv_w, lru_conv_b, lru_wa, lru_ba, lru_wx, lru_bx, lru_lambda, group_norm_g, w_out, moe_w_group, moe_w_expert, moe_w1, moe_w3, moe_w2):
    n_batch, s_lat, d = x.shape
    n_ctx = ctx.shape[1]
    depth = w_mod.shape[0]
    assert d == D_MODEL and n_ctx == TM and s_lat % TM == 0 and s_lat % GRID_W == 0
    n_lat = s_lat // TM
    nt_all = n_lat + 1
    seq = nt_all * TM
    mod_rows = -(-(n_batch + 1) // SUBLANES) * SUBLANES

    cc = jnp.concatenate([c, c_ctx[None, :], jnp.zeros((mod_rows - n_batch - 1, d), F32)], axis=0)
    mod_all = _modulation(cc, w_mod, b_mod)
    src = (x, ctx)
    cos_t, sin_t = _rope_tables(s_lat, seq)

    for l in range(depth):
        ctx_out = l < depth - 1
        nt = nt_all if ctx_out else n_lat
        mod = mod_all[l].reshape(mod_rows, 1, 6 * d)

        zm, zr, zh, zl = _inproj(src, mod, norm1_g[l].reshape(1, d), _pack_w_in(w_in[l]),
                                 n_batch, nt_all, n_lat)
        zm = zm.reshape(n_batch, seq, Z_MLA)
        zr = zr.reshape(n_batch, seq, Z_RET)
        zh = zh.reshape(n_batch, seq, Z_HY)
        zl = zl.reshape(n_batch, seq, Z_LRU)

        wq, wkv, gains = _pack_mla(mla_w_uq[l], mla_w_ukv[l], mla_q_norm_g[l], mla_kv_norm_g[l],
                                   mla_qn_g[l], mla_kn_g[l])
        tq = MLA_TQ if s_lat % MLA_TQ == 0 else TM
        y_mla = _mla(zm, cos_t, sin_t, gains, wq, wkv, n_batch, tq, 0, s_lat // tq, 0, nt_all)
        if ctx_out:
            y_mla = jnp.concatenate(
                [y_mla, _mla(zm, cos_t, sin_t, gains, wq, wkv, n_batch, TM, s_lat, 1, s_lat, 1)], axis=1)

        y_ret = _retention(zr, _retention_tables(ret_log_gamma[l].astype(F32)),
                           ret_norm_g[l].reshape(1, GROUP_W), n_batch, n_lat, ctx_out)

        s_hy, x0_hy = _hy_pre(zh, hy_conv_w[l], hy_conv_b[l].reshape(1, Z_HY), n_batch, n_lat)
        w1p = jnp.pad(hy_w1[l], ((0, LANES - hy_w1.shape[1]), (0, 0)))
        fargs = (w1p, hy_b1[l].reshape(1, -1), hy_w2[l], hy_b2[l].reshape(1, -1), hy_w3[l],
                 hy_freq[l].reshape(1, -1))
        hf, hb = _hy_filter(s_lat, *fargs)
        s_t = jnp.transpose(s_hy[:, :s_lat].reshape(n_batch, n_lat, TM, GROUP_W), (3, 1, 0, 2))
        y_t = _hy_conv(s_t, _toeplitz_rows(hf, hb, n_lat))
        y_hy = jnp.transpose(y_t, (2, 1, 3, 0)).reshape(n_batch, s_lat, GROUP_W)
        if ctx_out:
            hf_c, hb_c = _hy_filter(n_ctx, *fargs)
            sc_t = jnp.transpose(s_hy[:, s_lat:].reshape(n_batch, 1, TM, GROUP_W), (3, 1, 0, 2))
            yc_t = _hy_conv(sc_t, _toeplitz_rows(hf_c, hb_c, 1))
            y_hy_c = jnp.transpose(yc_t, (2, 1, 3, 0)).reshape(n_batch, n_ctx, GROUP_W)
        else:
            y_hy_c = jnp.zeros((n_batch, n_ctx, GROUP_W), BF16)
        y_hy = jnp.concatenate([y_hy, y_hy_c], axis=1)

        wg = jnp.concatenate([_block_diag(lru_wa[l, 0]), _block_diag(lru_wx[l, 0]),
                              _block_diag(lru_wa[l, 1]), _block_diag(lru_wx[l, 1])], axis=1).astype(BF16)
        bg = jnp.concatenate([lru_ba[l, 0], lru_bx[l, 0], lru_ba[l, 1], lru_bx[l, 1]]).reshape(1, -1)
        sp = jax.nn.softplus(-lru_lambda[l].astype(F32))
        y_lru = _lru(zl, lru_conv_w[l], lru_conv_b[l].reshape(1, -1), wg, bg, sp, n_batch, n_lat)

        wr = jnp.pad(jnp.concatenate([moe_w_group[l], moe_w_expert[l]], axis=1),
                     ((0, 0), (0, LANES - MOE_GROUPS - N_EXPERTS))).astype(BF16)
        xx, f, route = _merge(src, (y_mla, y_ret, y_hy, s_hy, x0_hy, y_lru), hy_d[l].reshape(1, -1),
                               group_norm_g[l].reshape(1, -1), w_out[l].astype(BF16), mod,
                               norm2_g[l].reshape(1, d), wr, n_batch, nt_all, nt, n_lat)

        ya, yb = _moe(f, route, moe_w1, moe_w3, moe_w2, l)
        xx = _combine(xx, ya, yb, route, mod, n_batch, nt_all, nt, n_lat, in_place=ctx_out)
        src = (xx,)

    return xx.reshape(n_batch, s_lat, d)
```
